```python
import math
import jax, jax.numpy as jnp
from jax import lax
import numpy as np

D_MODEL = 1024
BATCH = 4
SEQ = 4096
DEPTH = 4
DEC_BATCH = 32
DEC_SEQ = 8
PAST_LEN = 8192
PAGE_SIZE = 128

N_A_LAYERS = DEPTH // 2
N_B_LAYERS = DEPTH - N_A_LAYERS
D_RNN = D_MODEL
N_LRU_BLOCKS = 8
LRU_BLOCK = D_RNN // N_LRU_BLOCKS
CONV_WIDTH = 4
LRU_C = 8.0
N_HEADS = 8
HEAD_DIM = D_MODEL // (2 * N_HEADS)
V_DIM = 2 * HEAD_DIM
D_FF = 2816
ROPE_THETA = 10000.0
Q_BLOCK = 128
NORM_EPS = 1e-6
N_ADA = 9

kernel_name = "yoco_hawk_diff_attn_macaron_adaln"


def rms_norm(x, g):
    xf = x.astype(jnp.float32)
    y = xf * lax.rsqrt(jnp.mean(xf * xf, axis=-1, keepdims=True) + NORM_EPS)
    return (y * g.astype(jnp.float32)).astype(x.dtype)


def modulate(x, g, shift, scale):
    return rms_norm(x, g) * (1 + scale[:, None, :]) + shift[:, None, :]


def rope(x, pos):
    half = HEAD_DIM // 2
    inv = 1.0 / (ROPE_THETA ** (jnp.arange(half, dtype=jnp.float32) * (2.0 / HEAD_DIM)))
    ang = pos.astype(jnp.float32)[:, None] * inv[None, :]
    cos = jnp.cos(ang)[None, :, None, None, :]
    sin = jnp.sin(ang)[None, :, None, None, :]
    xf = x.astype(jnp.float32)
    x1, x2 = xf[..., :half], xf[..., half:]
    return jnp.concatenate([x1 * cos - x2 * sin, x2 * cos + x1 * sin], axis=-1).astype(x.dtype)


def swiglu(h, w_gu, w_d):
    gu = h @ w_gu
    g, u = gu[..., :D_FF], gu[..., D_FF:]
    return (jax.nn.silu(g) * u) @ w_d


def rglru_mixer(h, conv0, h0, w_in, conv_w, conv_b, w_rg, b_rg, w_ig, b_ig, lru_lambda, w_out):
    B, S, _ = h.shape
    xy = h @ w_in
    xb, yb = xy[..., :D_RNN], jax.nn.gelu(xy[..., D_RNN:])
    xp = jnp.concatenate([conv0.astype(xb.dtype), xb], axis=1)
    xc = conv_b + conv_w[0] * xp[:, 0:S]
    for j in range(1, CONV_WIDTH):
        xc = xc + conv_w[j] * xp[:, j:j + S]
    new_conv = xp[:, -(CONV_WIDTH - 1):]
    xblk = xc.reshape(B, S, N_LRU_BLOCKS, LRU_BLOCK)
    r = jax.nn.sigmoid(jnp.einsum('bsnc,ncd->bsnd', xblk, w_rg).reshape(B, S, D_RNN) + b_rg)
    i = jax.nn.sigmoid(jnp.einsum('bsnc,ncd->bsnd', xblk, w_ig).reshape(B, S, D_RNN) + b_ig)
    log_a = (-LRU_C * r.astype(jnp.float32)) * jax.nn.softplus(-lru_lambda.astype(jnp.float32))
    a = jnp.exp(log_a)
    u = jnp.sqrt(-jnp.expm1(2.0 * log_a)) * (i * xc).astype(jnp.float32)

    def combine(left, right):
        a_l, b_l = left
        a_r, b_r = right
        return a_r * a_l, a_r * b_l + b_r

    a_cum, b_cum = lax.associative_scan(combine, (a, u), axis=1)
    hs = a_cum * h0.astype(jnp.float32)[:, None, :] + b_cum
    out = (hs.astype(h.dtype) * yb) @ w_out
    return out, new_conv, hs[:, -1]


def diff_attn_mixer(h, pos, k_all, v_all, k_pos, w_q, lam_p, subln_g, w_o, lam_init):
    B, S, _ = h.shape
    q = (h @ w_q).reshape(B, S, N_HEADS, 2, HEAD_DIM)
    q = rope(q, pos) * (HEAD_DIM ** -0.5)
    lp = lam_p.astype(jnp.float32)
    lam = jnp.exp(jnp.sum(lp[0] * lp[1])) - jnp.exp(jnp.sum(lp[2] * lp[3])) + lam_init

    def block(args):
        qb, qpb = args
        s = jnp.einsum('bqhmd,bkhmd->bhmqk', qb, k_all).astype(jnp.float32)
        s = jnp.where(k_pos[None, :] <= qpb[:, None], s, -jnp.inf)
        p = jax.nn.softmax(s, axis=-1)
        att = p[:, :, 0] - lam * p[:, :, 1]
        return jnp.einsum('bhqk,bkhe->bqhe', att.astype(v_all.dtype), v_all)

    if S > Q_BLOCK and S % Q_BLOCK == 0:
        nb = S // Q_BLOCK
        qb = jnp.moveaxis(q.reshape(B, nb, Q_BLOCK, N_HEADS, 2, HEAD_DIM), 1, 0)
        o = lax.map(block, (qb, pos.reshape(nb, Q_BLOCK)))
        o = jnp.moveaxis(o, 0, 1).reshape(B, S, N_HEADS, V_DIM)
    else:
        o = block((q, pos))
    o = rms_norm(o, subln_g) * (1.0 - lam_init)
    return o.reshape(B, S, N_HEADS * V_DIM) @ w_o


def run_trunk(x, c, past_k, past_v, conv0, h0, params):
    (ada_w, ada_b, norm_g, ffn_w_gu, ffn_w_d,
     a_w_in, a_conv_w, a_conv_b, a_w_rg, a_b_rg, a_w_ig, a_b_ig, a_lambda, a_w_out,
     kv_ada_w, kv_ada_b, kv_norm_g, w_kv,
     b_w_q, b_lambda, b_subln_g, b_w_o, final_g) = params
    B, S, _ = x.shape
    past_len = past_k.shape[1]
    pos = past_len + jnp.arange(S, dtype=jnp.int32)
    k_pos = jnp.arange(past_len + S, dtype=jnp.int32)
    c_act = jax.nn.silu(c)
    new_conv, new_h = [], []
    k_new = v_new = k_all = v_all = None
    for l in range(DEPTH):
        if l == N_A_LAYERS:
            kvm = c_act @ kv_ada_w + kv_ada_b
            hkv = modulate(x, kv_norm_g, kvm[:, :D_MODEL], kvm[:, D_MODEL:])
            kv = hkv @ w_kv
            k_new = rope(kv[..., :N_HEADS * 2 * HEAD_DIM].reshape(B, S, N_HEADS, 2, HEAD_DIM), pos)
            v_new = kv[..., N_HEADS * 2 * HEAD_DIM:].reshape(B, S, N_HEADS, V_DIM)
            k_all = jnp.concatenate([past_k.astype(k_new.dtype), k_new], axis=1)
            v_all = jnp.concatenate([past_v.astype(v_new.dtype), v_new], axis=1)
        mod = (c_act @ ada_w[l] + ada_b[l]).reshape(B, N_ADA, D_MODEL)
        hh = modulate(x, norm_g[l, 0], mod[:, 0], mod[:, 1])
        x = x + 0.5 * mod[:, 2, None] * swiglu(hh, ffn_w_gu[l, 0], ffn_w_d[l, 0])
        hh = modulate(x, norm_g[l, 1], mod[:, 3], mod[:, 4])
        if l < N_A_LAYERS:
            m, cs, hs = rglru_mixer(hh, conv0[l], h0[l], a_w_in[l], a_conv_w[l], a_conv_b[l],
                                    a_w_rg[l], a_b_rg[l], a_w_ig[l], a_b_ig[l], a_lambda[l], a_w_out[l])
            new_conv.append(cs)
            new_h.append(hs)
        else:
            j = l - N_A_LAYERS
            lam_init = 0.8 - 0.6 * math.exp(-0.3 * l)
            m = diff_attn_mixer(hh, pos, k_all, v_all, k_pos, b_w_q[j], b_lambda[j], b_subln_g[j],
                                b_w_o[j], lam_init)
        x = x + mod[:, 5, None] * m
        hh = modulate(x, norm_g[l, 2], mod[:, 6], mod[:, 7])
        x = x + 0.5 * mod[:, 8, None] * swiglu(hh, ffn_w_gu[l, 1], ffn_w_d[l, 1])
    y = rms_norm(x, final_g)
    return y, jnp.stack(new_conv, 0), jnp.stack(new_h, 0), k_new, v_new


def setup_inputs(seed: int = 0) -> dict:
    key = jax.random.key(seed)
    ks = iter(jax.random.split(key, 48))
    f32 = jnp.float32

    def nrm(shape, scale):
        return jax.random.normal(next(ks), shape, f32) * scale

    n_pages = PAST_LEN // PAGE_SIZE
    n_used = DEC_BATCH * n_pages
    n_phys = n_used + n_used // 4
    page_table = jax.random.permutation(next(ks), n_phys)[:n_used].reshape(DEC_BATCH, n_pages).astype(jnp.int32)

    u = jax.random.uniform(next(ks), (N_A_LAYERS, D_RNN), f32, minval=0.9, maxval=0.999)
    s = u ** (1.0 / LRU_C)
    a_lambda = jnp.log(s) - jnp.log1p(-s)

    return {
        "x_prompt": nrm((BATCH, SEQ, D_MODEL), 1.0),
        "x_sample": nrm((DEC_BATCH, DEC_SEQ, D_MODEL), 1.0),
        "c_prompt": nrm((BATCH, D_MODEL), 1.0),
        "c_sample": nrm((DEC_BATCH, D_MODEL), 1.0),
        "state_conv": nrm((N_A_LAYERS, DEC_BATCH, CONV_WIDTH - 1, D_RNN), 1.0),
        "state_h": nrm((N_A_LAYERS, DEC_BATCH, D_RNN), 0.5),
        "cache_k": nrm((n_phys, PAGE_SIZE, N_HEADS, 2, HEAD_DIM), 1.0),
        "cache_v": nrm((n_phys, PAGE_SIZE, N_HEADS, V_DIM), 1.0),
        "page_table": page_table,
        "ada_w": nrm((DEPTH, D_MODEL, N_ADA * D_MODEL), D_MODEL ** -0.5),
        "ada_b": nrm((DEPTH, N_ADA * D_MODEL), 0.01),
        "norm_g": 1.0 + nrm((DEPTH, 3, D_MODEL), 0.05),
        "ffn_w_gu": nrm((DEPTH, 2, D_MODEL, 2 * D_FF), D_MODEL ** -0.5),
        "ffn_w_d": nrm((DEPTH, 2, D_FF, D_MODEL), D_FF ** -0.5),
        "a_w_in": nrm((N_A_LAYERS, D_MODEL, 2 * D_RNN), D_MODEL ** -0.5),
        "a_conv_w": nrm((N_A_LAYERS, CONV_WIDTH, D_RNN), CONV_WIDTH ** -0.5),
        "a_conv_b": nrm((N_A_LAYERS, D_RNN), 0.01),
        "a_w_rg": nrm((N_A_LAYERS, N_LRU_BLOCKS, LRU_BLOCK, LRU_BLOCK), LRU_BLOCK ** -0.5),
        "a_b_rg": nrm((N_A_LAYERS, D_RNN), 0.01),
        "a_w_ig": nrm((N_A_LAYERS, N_LRU_BLOCKS, LRU_BLOCK, LRU_BLOCK), LRU_BLOCK ** -0.5),
        "a_b_ig": nrm((N_A_LAYERS, D_RNN), 0.01),
        "a_lambda": a_lambda,
        "a_w_out": nrm((N_A_LAYERS, D_RNN, D_MODEL), D_RNN ** -0.5),
        "kv_ada_w": nrm((D_MODEL, 2 * D_MODEL), D_MODEL ** -0.5),
        "kv_ada_b": nrm((2 * D_MODEL,), 0.01),
        "kv_norm_g": 1.0 + nrm((D_MODEL,), 0.05),
        "w_kv": nrm((D_MODEL, N_HEADS * 2 * HEAD_DIM + N_HEADS * V_DIM), D_MODEL ** -0.5),
        "b_w_q": nrm((N_B_LAYERS, D_MODEL, N_HEADS * 2 * HEAD_DIM), D_MODEL ** -0.5),
        "b_lambda": nrm((N_B_LAYERS, 4, HEAD_DIM), 0.1),
        "b_subln_g": 1.0 + nrm((N_B_LAYERS, V_DIM), 0.05),
        "b_w_o": nrm((N_B_LAYERS, N_HEADS * V_DIM, D_MODEL), (N_HEADS * V_DIM) ** -0.5),
        "final_g": 1.0 + nrm((D_MODEL,), 0.05),
    }


def reference(x_prompt, x_sample, c_prompt, c_sample, state_conv, state_h, cache_k, cache_v, page_table,
              ada_w, ada_b, norm_g, ffn_w_gu, ffn_w_d,
              a_w_in, a_conv_w, a_conv_b, a_w_rg, a_b_rg, a_w_ig, a_b_ig, a_lambda, a_w_out,
              kv_ada_w, kv_ada_b, kv_norm_g, w_kv,
              b_w_q, b_lambda, b_subln_g, b_w_o, final_g):
    params = (ada_w, ada_b, norm_g, ffn_w_gu, ffn_w_d,
              a_w_in, a_conv_w, a_conv_b, a_w_rg, a_b_rg, a_w_ig, a_b_ig, a_lambda, a_w_out,
              kv_ada_w, kv_ada_b, kv_norm_g, w_kv,
              b_w_q, b_lambda, b_subln_g, b_w_o, final_g)
    bp = x_prompt.shape[0]
    zk = jnp.zeros((bp, 0, N_HEADS, 2, HEAD_DIM), x_prompt.dtype)
    zv = jnp.zeros((bp, 0, N_HEADS, V_DIM), x_prompt.dtype)
    zc = jnp.zeros((N_A_LAYERS, bp, CONV_WIDTH - 1, D_RNN), x_prompt.dtype)
    zh = jnp.zeros((N_A_LAYERS, bp, D_RNN), jnp.float32)
    y_prompt, conv_prompt, h_prompt, k_prompt, v_prompt = run_trunk(x_prompt, c_prompt, zk, zv, zc, zh, params)
    bs = x_sample.shape[0]
    n_pages = page_table.shape[1]
    past_k = cache_k[page_table].reshape(bs, n_pages * PAGE_SIZE, N_HEADS, 2, HEAD_DIM)
    past_v = cache_v[page_table].reshape(bs, n_pages * PAGE_SIZE, N_HEADS, V_DIM)
    y_sample, conv_sample, h_sample, k_sample, v_sample = run_trunk(
        x_sample, c_sample, past_k, past_v, state_conv, state_h, params)
    return (y_prompt, y_sample, conv_prompt, h_prompt, k_prompt, v_prompt,
            conv_sample, h_sample, k_sample, v_sample)
```

```python
import functools
import math

import jax
import jax.numpy as jnp
from jax import lax
from jax.experimental import pallas as pl
from jax.experimental.pallas import tpu as pltpu

D_MODEL = 1024
DEPTH = 4
N_A_LAYERS = DEPTH // 2
D_RNN = D_MODEL
N_LRU_BLOCKS = 8
LRU_BLOCK = D_RNN // N_LRU_BLOCKS
CONV_WIDTH = 4
LRU_C = 8.0
N_HEADS = 8
HEAD_DIM = D_MODEL // (2 * N_HEADS)
V_DIM = 2 * HEAD_DIM
D_FF = 2816
ROPE_THETA = 10000.0
NORM_EPS = 1e-6
N_ADA = 9
PAGE_SIZE = 128

SUBLANES = 8
LANES = 128
VMEM_LIMIT_BYTES = 56 * 1024 * 1024

F32 = jnp.float32
BF16 = jnp.bfloat16


def _params(*sem):
    return pltpu.CompilerParams(dimension_semantics=sem, vmem_limit_bytes=VMEM_LIMIT_BYTES)


def _const_spec(shape, index_map):
    return pl.BlockSpec(shape, index_map, pipeline_mode=pl.Buffered(1))


def _modulate(x, g, shift, scale):
    ms = jnp.mean(x * x, axis=-1, keepdims=True)
    y = x * lax.rsqrt(ms + NORM_EPS)
    return (y * g) * (1.0 + scale) + shift


def _ada_kernel(c_ref, w_ref, b_ref, o_ref):
    c_act = jax.nn.silu(c_ref[...]).astype(BF16)
    o_ref[...] = jnp.dot(c_act, w_ref[...].astype(BF16), preferred_element_type=F32) + b_ref[...]


def _ada_call(c_all, w, b, tn):
    nl, _, n = w.shape
    r = c_all.shape[0]
    return pl.pallas_call(
        _ada_kernel,
        grid=(nl, n // tn),
        in_specs=[
            pl.BlockSpec((r, D_MODEL), lambda l, j: (0, 0)),
            pl.BlockSpec((None, D_MODEL, tn), lambda l, j: (l, 0, j)),
            pl.BlockSpec((None, 1, tn), lambda l, j: (l, 0, j)),
        ],
        out_specs=pl.BlockSpec((None, r, tn), lambda l, j: (l, 0, j)),
        out_shape=jax.ShapeDtypeStruct((nl, r, n), F32),
        compiler_params=_params("arbitrary", "arbitrary"),
        name="ada_rows",
    )(c_all, w, b)


def _ffn_kernel(x_ref, sh_ref, sc_ref, gt_ref, g_ref, wgu_ref, wd_ref, *rest, final):
    if final:
        fg_ref, o_ref = rest
    else:
        (o_ref,) = rest
    x = x_ref[...]
    nb, t, _ = x.shape
    h = _modulate(x, g_ref[...], sh_ref[...], sc_ref[...])
    hb = h.reshape(nb * t, D_MODEL).astype(BF16)
    gu = jnp.dot(hb, wgu_ref[...], preferred_element_type=F32)
    a = (jax.nn.silu(gu[:, :D_FF]) * gu[:, D_FF:]).astype(BF16)
    y = jnp.dot(a, wd_ref[...], preferred_element_type=F32).reshape(nb, t, D_MODEL)
    out = x + (0.5 * gt_ref[...]) * y
    if final:
        ms = jnp.mean(out * out, axis=-1, keepdims=True)
        out = (out * lax.rsqrt(ms + NORM_EPS)) * fg_ref[...]
    o_ref[...] = out


def _mod_spec(row, nb):
    return pl.BlockSpec((None, nb, 1, D_MODEL), lambda b, t: (row, b, 0, 0))


def _tok_spec(nb, t, width=D_MODEL):
    return pl.BlockSpec((nb, t, width), lambda b, i: (b, i, 0))


def _ffn_call(x, mods, row0, norm_g3, g_row, wgu, wd, l, j, nb, t, final_g=None):
    bsz, s, _ = x.shape
    final = final_g is not None
    in_specs = [
        _tok_spec(nb, t),
        _mod_spec(row0, nb), _mod_spec(row0 + 1, nb), _mod_spec(row0 + 2, nb),
        pl.BlockSpec((None, 1, D_MODEL), lambda b, i: (g_row, 0, 0)),
        _const_spec((None, None, D_MODEL, 2 * D_FF), lambda b, i: (l, j, 0, 0)),
        _const_spec((None, None, D_FF, D_MODEL), lambda b, i: (l, j, 0, 0)),
    ]
    args = [x, mods, mods, mods, norm_g3, wgu, wd]
    if final:
        in_specs.append(pl.BlockSpec((1, D_MODEL), lambda b, i: (0, 0)))
        args.append(final_g)
    return pl.pallas_call(
        functools.partial(_ffn_kernel, final=final),
        grid=(bsz // nb, s // t),
        in_specs=in_specs,
        out_specs=_tok_spec(nb, t),
        out_shape=jax.ShapeDtypeStruct(x.shape, F32),
        compiler_params=_params("arbitrary", "arbitrary"),
        name="ffn",
    )(*args)


def _shift_rows(x, tail, j, row8):
    t = x.shape[1]
    r = pltpu.roll(x, j, axis=1)
    head = jnp.where(row8 < j, pltpu.roll(tail, j, axis=1), r[:, :SUBLANES])
    if t == SUBLANES:
        return head
    return jnp.concatenate([head, r[:, SUBLANES:]], axis=1)


def _rglru_kernel(x_ref, sh_ref, sc_ref, gt_ref, g_ref, conv0_ref, h0_ref,
                  win_ref, cw_ref, cb_ref, wg_ref, brg_ref, big_ref, lam_ref, wout_ref,
                  xo_ref, convo_ref, ho_ref,
                  tail_sc, h_sc, a_sc, u_sc, hs_sc):
    @pl.when(pl.program_id(1) == 0)
    def _():
        tail_sc[...] = conv0_ref[...]
        h_sc[...] = h0_ref[...]

    x = x_ref[...]
    nb, t, _ = x.shape
    m = nb * t
    h = _modulate(x, g_ref[...], sh_ref[...], sc_ref[...])
    xy = jnp.dot(h.reshape(m, D_MODEL).astype(BF16), win_ref[...], preferred_element_type=F32)
    xb = xy[:, :D_RNN].reshape(nb, t, D_RNN)
    yb = jax.nn.gelu(xy[:, D_RNN:])

    tail = tail_sc[...]
    row8 = lax.broadcasted_iota(jnp.int32, (1, SUBLANES, 1), 1)
    cw = cw_ref[...]
    xc = cb_ref[...] + cw[0:1] * _shift_rows(xb, tail, 3, row8)
    xc = xc + cw[1:2] * _shift_rows(xb, tail, 2, row8)
    xc = xc + cw[2:3] * _shift_rows(xb, tail, 1, row8)
    xc = xc + cw[3:4] * xb
    new_tail = xb[:, t - SUBLANES:, :]
    tail_sc[...] = new_tail
    convo_ref[...] = new_tail

    xc2 = xc.reshape(m, D_RNN)
    xcb = xc2.astype(BF16)
    pre = [jnp.dot(xcb[:, n * LRU_BLOCK:(n + 1) * LRU_BLOCK], wg_ref[n], preferred_element_type=F32)
           for n in range(N_LRU_BLOCKS)]
    rpre = jnp.concatenate([p[:, :LRU_BLOCK] for p in pre], axis=1)
    ipre = jnp.concatenate([p[:, LRU_BLOCK:] for p in pre], axis=1)
    r = jax.nn.sigmoid(rpre + brg_ref[...])
    i = jax.nn.sigmoid(ipre + big_ref[...])
    log_a = (-LRU_C * r) * jax.nn.softplus(-lam_ref[...])
    a_sc[...] = jnp.exp(log_a).reshape(nb, t, D_RNN)
    th = jnp.tanh(log_a)
    one_minus_a2 = (-2.0 * th) / (1.0 - th)
    u_sc[...] = (jnp.sqrt(one_minus_a2) * (i * xc2)).reshape(nb, t, D_RNN)

    def group(gi, hprev):
        s0 = pl.multiple_of(gi * SUBLANES, SUBLANES)
        ca = a_sc[:, pl.ds(s0, SUBLANES), :]
        cb = u_sc[:, pl.ds(s0, SUBLANES), :]
        for s in (1, 2, 4):
            keep = row8 >= s
            cb = jnp.where(keep, ca * pltpu.roll(cb, s, axis=1) + cb, cb)
            ca = jnp.where(keep, ca * pltpu.roll(ca, s, axis=1), ca)
        hs = ca * hprev + cb
        hs_sc[:, pl.ds(s0, SUBLANES), :] = hs
        return hs[:, SUBLANES - 1:SUBLANES, :]

    hlast = lax.fori_loop(0, t // SUBLANES, group, h_sc[...])
    h_sc[...] = hlast
    ho_ref[...] = hlast

    mixed = (hs_sc[...].reshape(m, D_RNN) * yb).astype(BF16)
    out = jnp.dot(mixed, wout_ref[...], preferred_element_type=F32).reshape(nb, t, D_MODEL)
    xo_ref[...] = x + gt_ref[...] * out


def _rglru_call(x, mods, row0, norm_g3, g_row, conv0, h0, w_in, conv_w, conv_b, w_gate, b_rg, b_ig,
                lam, w_out, l, nb, t):
    bsz, s, _ = x.shape
    vec = lambda: pl.BlockSpec((None, 1, D_RNN), lambda b, i: (l, 0, 0))
    return pl.pallas_call(
        _rglru_kernel,
        grid=(bsz // nb, s // t),
        in_specs=[
            _tok_spec(nb, t),
            _mod_spec(row0, nb), _mod_spec(row0 + 1, nb), _mod_spec(row0 + 2, nb),
            pl.BlockSpec((None, 1, D_MODEL), lambda b, i: (g_row, 0, 0)),
            pl.BlockSpec((None, nb, SUBLANES, D_RNN), lambda b, i: (l, b, 0, 0)),
            pl.BlockSpec((None, nb, 1, D_RNN), lambda b, i: (l, b, 0, 0)),
            _const_spec((None, D_MODEL, 2 * D_RNN), lambda b, i: (l, 0, 0)),
            pl.BlockSpec((None, CONV_WIDTH, D_RNN), lambda b, i: (l, 0, 0)),
            vec(),
            _const_spec((None, N_LRU_BLOCKS, LRU_BLOCK, 2 * LRU_BLOCK), lambda b, i: (l, 0, 0, 0)),
            vec(), vec(), vec(),
            _const_spec((None, D_RNN, D_MODEL), lambda b, i: (l, 0, 0)),
        ],
        out_specs=[
            _tok_spec(nb, t),
            pl.BlockSpec((nb, SUBLANES, D_RNN), lambda b, i: (b, 0, 0)),
            pl.BlockSpec((nb, 1, D_RNN), lambda b, i: (b, 0, 0)),
        ],
        out_shape=[
            jax.ShapeDtypeStruct(x.shape, F32),
            jax.ShapeDtypeStruct((bsz, SUBLANES, D_RNN), F32),
            jax.ShapeDtypeStruct((bsz, 1, D_RNN), F32),
        ],
        scratch_shapes=[
            pltpu.VMEM((nb, SUBLANES, D_RNN), F32),
            pltpu.VMEM((nb, 1, D_RNN), F32),
            pltpu.VMEM((nb, t, D_RNN), F32),
            pltpu.VMEM((nb, t, D_RNN), F32),
            pltpu.VMEM((nb, t, D_RNN), F32),
        ],
        compiler_params=_params("arbitrary", "arbitrary"),
        name="rglru",
    )(x, mods, mods, mods, norm_g3, conv0, h0, w_in, conv_w, conv_b, w_gate, b_rg, b_ig, lam, w_out)


def _rope(x, cos_ref, sin_ref):
    reps = D_MODEL // LANES
    cos = jnp.concatenate([cos_ref[...]] * reps, axis=-1)[None]
    sin = jnp.concatenate([sin_ref[...]] * reps, axis=-1)[None]
    lane = lax.broadcasted_iota(jnp.int32, (1, 1, D_MODEL), 2)
    low_half = (lane % HEAD_DIM) < (HEAD_DIM // 2)
    swapped = jnp.where(low_half,
                        pltpu.roll(x, D_MODEL - HEAD_DIM // 2, axis=2),
                        pltpu.roll(x, HEAD_DIM // 2, axis=2))
    return x * cos + swapped * sin


def _kv_kernel(x_ref, sh_ref, sc_ref, g_ref, w_ref, cos_ref, sin_ref, k_ref, v_ref, kb_ref, vb_ref):
    x = x_ref[...]
    nb, t, _ = x.shape
    h = _modulate(x, g_ref[...], sh_ref[...], sc_ref[...])
    y = jnp.dot(h.reshape(nb * t, D_MODEL).astype(BF16), w_ref[...], preferred_element_type=F32)
    k = _rope(y[:, :D_MODEL].reshape(nb, t, D_MODEL), cos_ref, sin_ref)
    v = y[:, D_MODEL:].reshape(nb, t, D_MODEL)
    k_ref[...] = k
    v_ref[...] = v
    kb_ref[...] = k.astype(BF16)
    vb_ref[...] = v.astype(BF16)


def _q_kernel(x_ref, sh_ref, sc_ref, g_ref, w_ref, cos_ref, sin_ref, q_ref):
    x = x_ref[...]
    nb, t, _ = x.shape
    h = _modulate(x, g_ref[...], sh_ref[...], sc_ref[...])
    y = jnp.dot(h.reshape(nb * t, D_MODEL).astype(BF16), w_ref[...], preferred_element_type=F32)
    q = _rope(y.reshape(nb, t, D_MODEL), cos_ref, sin_ref) * (HEAD_DIM ** -0.5)
    q_ref[...] = q.astype(q_ref.dtype)


def _proj_specs(mods, row0, g_spec, w_spec, nb, t):
    rope_spec = pl.BlockSpec((t, LANES), lambda b, i: (i, 0))
    return [_tok_spec(nb, t), _mod_spec(row0, nb), _mod_spec(row0 + 1, nb), g_spec, w_spec, rope_spec, rope_spec]


def _kv_call(x, kvmods, kv_norm_g, w_kv, cos, sin, nb, t):
    bsz, s, _ = x.shape
    g_spec = pl.BlockSpec((1, D_MODEL), lambda b, i: (0, 0))
    w_spec = _const_spec((D_MODEL, 2 * D_MODEL), lambda b, i: (0, 0))
    shp = lambda dt: jax.ShapeDtypeStruct(x.shape, dt)
    return pl.pallas_call(
        _kv_kernel,
        grid=(bsz // nb, s // t),
        in_specs=_proj_specs(kvmods, 0, g_spec, w_spec, nb, t),
        out_specs=[_tok_spec(nb, t)] * 4,
        out_shape=[shp(F32), shp(F32), shp(BF16), shp(BF16)],
        compiler_params=_params("arbitrary", "arbitrary"),
        name="kv_proj",
    )(x, kvmods, kvmods, kv_norm_g, w_kv, cos, sin)


def _q_call(x, mods, row0, norm_g3, g_row, w_q, j, cos, sin, nb, t, out_dtype):
    bsz, s, _ = x.shape
    g_spec = pl.BlockSpec((None, 1, D_MODEL), lambda b, i: (g_row, 0, 0))
    w_spec = _const_spec((None, D_MODEL, D_MODEL), lambda b, i: (j, 0, 0))
    return pl.pallas_call(
        _q_kernel,
        grid=(bsz // nb, s // t),
        in_specs=_proj_specs(mods, row0, g_spec, w_spec, nb, t),
        out_specs=_tok_spec(nb, t),
        out_shape=jax.ShapeDtypeStruct(x.shape, out_dtype),
        compiler_params=_params("arbitrary", "arbitrary"),
        name="q_proj",
    )(x, mods, mods, norm_g3, w_q, cos, sin)


def _diff_lambda(lamp_ref, lam_init):
    lp = lamp_ref[...]
    e1 = jnp.exp(jnp.sum(lp[0:1] * lp[1:2], axis=-1, keepdims=True))
    e2 = jnp.exp(jnp.sum(lp[2:3] * lp[3:4], axis=-1, keepdims=True))
    return e1 - e2 + lam_init


def _subln(d, sg, lam_init):
    ms = jnp.mean(d * d, axis=-1, keepdims=True)
    return ((d * lax.rsqrt(ms + NORM_EPS)) * sg) * (1.0 - lam_init)


def _attn_prompt_kernel(q_ref, k_ref, v_ref, lamp_ref, sg_ref, o_ref, m_sc, l_sc, acc_sc, *, lam_init, tq):
    qi = pl.program_id(2)
    q = q_ref[...]
    lane = lax.broadcasted_iota(jnp.int32, (1, V_DIM), 1)
    zero = jnp.zeros_like(q)
    qq = jnp.concatenate([jnp.where(lane < HEAD_DIM, q, zero), jnp.where(lane >= HEAD_DIM, q, zero)], axis=0)
    m_sc[...] = jnp.full(m_sc.shape, -jnp.inf, F32)
    l_sc[...] = jnp.zeros(l_sc.shape, F32)
    acc_sc[...] = jnp.zeros(acc_sc.shape, F32)

    def step(off, masked):
        kb = k_ref[pl.ds(off, tq), :]
        vb = v_ref[pl.ds(off, tq), :]
        s = lax.dot_general(qq, kb, (((1,), (1,)), ((), ())), preferred_element_type=F32)
        if masked:
            r = lax.broadcasted_iota(jnp.int32, (tq, tq), 0)
            c = lax.broadcasted_iota(jnp.int32, (tq, tq), 1)
            ok = c <= r
            s = jnp.where(jnp.concatenate([ok, ok], axis=0), s, -jnp.inf)
        m_prev = m_sc[...]
        m_new = jnp.maximum(m_prev, jnp.max(s, axis=-1, keepdims=True))
        alpha = jnp.exp(m_prev - m_new)
        p = jnp.exp(s - m_new)
        l_sc[...] = alpha * l_sc[...] + jnp.sum(p, axis=-1, keepdims=True)
        acc_sc[...] = alpha * acc_sc[...] + jnp.dot(p.astype(BF16), vb, preferred_element_type=F32)
        m_sc[...] = m_new

    def body(j, carry):
        step(pl.multiple_of(j * tq, tq), False)
        return carry

    lax.fori_loop(0, qi, body, 0)
    step(pl.multiple_of(qi * tq, tq), True)

    o = acc_sc[...] / l_sc[...]
    lam = _diff_lambda(lamp_ref, lam_init)
    d = o[:tq] - lam * o[tq:]
    o_ref[...] = _subln(d, sg_ref[...], lam_init).astype(o_ref.dtype)


def _attn_prompt_call(q, k, v, lam_p, subln_g, j, lam_init, tq):
    bsz, s, _ = q.shape
    return pl.pallas_call(
        functools.partial(_attn_prompt_kernel, lam_init=lam_init, tq=tq),
        grid=(bsz, N_HEADS, s // tq),
        in_specs=[
            pl.BlockSpec((None, tq, V_DIM), lambda b, h, i: (b, i, h)),
            pl.BlockSpec((None, s, V_DIM), lambda b, h, i: (b, 0, h)),
            pl.BlockSpec((None, s, V_DIM), lambda b, h, i: (b, 0, h)),
            pl.BlockSpec((None, 4, HEAD_DIM), lambda b, h, i: (j, 0, 0)),
            pl.BlockSpec((None, 1, V_DIM), lambda b, h, i: (j, 0, 0)),
        ],
        out_specs=pl.BlockSpec((None, tq, V_DIM), lambda b, h, i: (b, i, h)),
        out_shape=jax.ShapeDtypeStruct(q.shape, BF16),
        scratch_shapes=[
            pltpu.VMEM((2 * tq, 1), F32),
            pltpu.VMEM((2 * tq, 1), F32),
            pltpu.VMEM((2 * tq, V_DIM), F32),
        ],
        compiler_params=_params("arbitrary", "arbitrary", "arbitrary"),
        name="attn_prompt",
    )(q, k, v, lam_p, subln_g)


def _attn_decode_kernel(pt_ref, q_ref, *refs, pps, nstep, lam_init):
    k_refs = refs[:pps]
    v_refs = refs[pps:2 * pps]
    kn_ref, vn_ref, lamp_ref, sg_ref, o_ref, m_sc, l_sc, acc_sc = refs[2 * pps:]
    step_id = pl.program_id(1)
    nrow = 2 * N_HEADS * SUBLANES

    @pl.when(step_id == 0)
    def _():
        m_sc[...] = jnp.full(m_sc.shape, -jnp.inf, F32)
        l_sc[...] = jnp.zeros(l_sc.shape, F32)
        acc_sc[...] = jnp.zeros(acc_sc.shape, F32)

    q = q_ref[...]
    row = lax.broadcasted_iota(jnp.int32, (nrow, D_MODEL), 0)
    col = lax.broadcasted_iota(jnp.int32, (nrow, D_MODEL), 1)
    wt = jnp.where((row // SUBLANES) == (col // HEAD_DIM),
                   jnp.concatenate([q] * (nrow // SUBLANES), axis=0), 0.0).astype(BF16)

    def update(s, vs):
        m_prev = m_sc[...]
        m_new = jnp.maximum(m_prev, jnp.max(s, axis=-1, keepdims=True))
        alpha = jnp.exp(m_prev - m_new)
        p = jnp.exp(s - m_new)
        l_sc[...] = alpha * l_sc[...] + jnp.sum(p, axis=-1, keepdims=True)
        p = p.astype(BF16)
        pv = jnp.dot(p[:, :PAGE_SIZE], vs[0], preferred_element_type=F32)
        for i in range(1, len(vs)):
            pv = pv + jnp.dot(p[:, i * PAGE_SIZE:(i + 1) * PAGE_SIZE], vs[i], preferred_element_type=F32)
        acc_sc[...] = alpha * acc_sc[...] + pv
        m_sc[...] = m_new

    nt = (((1,), (1,)), ((), ()))

    @pl.when(step_id < nstep)
    def _():
        ss = [lax.dot_general(wt, k_refs[i][...].astype(BF16), nt, preferred_element_type=F32)
              for i in range(pps)]
        update(jnp.concatenate(ss, axis=1), [v_refs[i][...].astype(BF16) for i in range(pps)])

    @pl.when(step_id == nstep)
    def _():
        s = lax.dot_general(wt, kn_ref[...], nt, preferred_element_type=F32)
        r = lax.broadcasted_iota(jnp.int32, (nrow, PAGE_SIZE), 0)
        c = lax.broadcasted_iota(jnp.int32, (nrow, PAGE_SIZE), 1)
        update(jnp.where(c <= (r % SUBLANES), s, -jnp.inf), [vn_ref[...]])
        o = acc_sc[...] / l_sc[...]
        lam = _diff_lambda(lamp_ref, lam_init)
        heads = []
        for hd in range(N_HEADS):
            r0 = hd * 2 * SUBLANES
            c0 = hd * V_DIM
            d = o[r0:r0 + SUBLANES, c0:c0 + V_DIM] - lam * o[r0 + SUBLANES:r0 + 2 * SUBLANES, c0:c0 + V_DIM]
            heads.append(_subln(d, sg_ref[...], lam_init))
        o_ref[...] = jnp.concatenate(heads, axis=1)


def _attn_decode_call(q, cache_k, cache_v, page_table, kn_pad, vn_pad, lam_p, subln_g, j, lam_init, pps):
    bsz = q.shape[0]
    n_pages = page_table.shape[1]
    nstep = n_pages // pps

    def page_spec(i):
        return pl.BlockSpec(
            (None, PAGE_SIZE, D_MODEL),
            lambda b, s, pt: (pt[b, jnp.minimum(s * pps + i, n_pages - 1)], 0, 0))

    tok = pl.BlockSpec((None, SUBLANES, D_MODEL), lambda b, s, pt: (b, 0, 0))
    new = pl.BlockSpec((None, PAGE_SIZE, D_MODEL), lambda b, s, pt: (b, 0, 0))
    nrow = 2 * N_HEADS * SUBLANES
    grid_spec = pltpu.PrefetchScalarGridSpec(
        num_scalar_prefetch=1,
        grid=(bsz, nstep + 1),
        in_specs=[tok] + [page_spec(i) for i in range(pps)] * 2 + [
            new, new,
            pl.BlockSpec((None, 4, HEAD_DIM), lambda b, s, pt: (j, 0, 0)),
            pl.BlockSpec((None, 1, V_DIM), lambda b, s, pt: (j, 0, 0)),
        ],
        out_specs=tok,
        scratch_shapes=[
            pltpu.VMEM((nrow, 1), F32),
            pltpu.VMEM((nrow, 1), F32),
            pltpu.VMEM((nrow, D_MODEL), F32),
        ],
    )
    return pl.pallas_call(
        functools.partial(_attn_decode_kernel, pps=pps, nstep=nstep, lam_init=lam_init),
        grid_spec=grid_spec,
        out_shape=jax.ShapeDtypeStruct(q.shape, F32),
        compiler_params=_params("arbitrary", "arbitrary"),
        name="attn_decode",
    )(page_table, q, *([cache_k] * pps), *([cache_v] * pps), kn_pad, vn_pad, lam_p, subln_g)


def _oproj_kernel(x_ref, o_ref, gt_ref, w_ref, xo_ref):
    x = x_ref[...]
    nb, t, _ = x.shape
    y = jnp.dot(o_ref[...].reshape(nb * t, D_MODEL).astype(BF16), w_ref[...], preferred_element_type=F32)
    xo_ref[...] = x + gt_ref[...] * y.reshape(nb, t, D_MODEL)


def _oproj_call(x, o, mods, row, w_o, j, nb, t):
    bsz, s, _ = x.shape
    return pl.pallas_call(
        _oproj_kernel,
        grid=(bsz // nb, s // t),
        in_specs=[
            _tok_spec(nb, t), _tok_spec(nb, t), _mod_spec(row, nb),
            _const_spec((None, D_MODEL, D_MODEL), lambda b, i: (j, 0, 0)),
        ],
        out_specs=_tok_spec(nb, t),
        out_shape=jax.ShapeDtypeStruct(x.shape, F32),
        compiler_params=_params("arbitrary", "arbitrary"),
        name="o_proj",
    )(x, o, mods, w_o)


PROMPT_TILE = 512
ATTN_TILE = 512
PAGES_PER_STEP = 4
C_ROWS_PROMPT = 0
C_ROWS_SAMPLE = 8


def _rope_tables(pos):
    half = HEAD_DIM // 2
    inv = 1.0 / (ROPE_THETA ** (jnp.arange(half, dtype=F32) * (2.0 / HEAD_DIM)))
    ang = pos.astype(F32)[:, None] * inv[None, :]
    cos = jnp.tile(jnp.cos(ang), (1, LANES // half))
    sign = jnp.where((jnp.arange(LANES) % HEAD_DIM) < half, -1.0, 1.0).astype(F32)
    sin = jnp.tile(jnp.sin(ang), (1, LANES // half)) * sign[None, :]
    return cos, sin


def _group_rows(rows, lo, n):
    nl, _, kd = rows.shape
    k = kd // D_MODEL
    g = rows[:, lo:lo + n].reshape(nl, n, k, D_MODEL)
    return jnp.transpose(g, (0, 2, 1, 3)).reshape(nl * k, n, 1, D_MODEL)


def _run_trunk(x, mods, kvmods, conv0, h0, past_len, w, nb, t, attn_fn):
    bsz, s, _ = x.shape
    pos = past_len + jnp.arange(s, dtype=jnp.int32)
    cos, sin = _rope_tables(pos)
    convs, hs = [], []
    k_new = v_new = kb = vb = None
    for l in range(DEPTH):
        if l == N_A_LAYERS:
            k_new, v_new, kb, vb = _kv_call(x, kvmods, w["kv_norm_g"], w["w_kv"], cos, sin, nb, t)
        r0 = l * N_ADA
        x = _ffn_call(x, mods, r0, w["norm_g"], l * 3, w["ffn_w_gu"], w["ffn_w_d"], l, 0, nb, t)
        if l < N_A_LAYERS:
            x, cs, hl = _rglru_call(x, mods, r0 + 3, w["norm_g"], l * 3 + 1, conv0, h0,
                                    w["a_w_in"], w["a_conv_w"], w["a_conv_b"], w["a_w_gate"],
                                    w["a_b_rg"], w["a_b_ig"], w["a_lambda"], w["a_w_out"], l, nb, t)
            convs.append(cs[:, SUBLANES - (CONV_WIDTH - 1):])
            hs.append(hl[:, 0])
        else:
            j = l - N_A_LAYERS
            lam_init = 0.8 - 0.6 * math.exp(-0.3 * l)
            o = attn_fn(x, mods, r0 + 3, l * 3 + 1, j, lam_init, cos, sin, kb, vb)
            x = _oproj_call(x, o, mods, r0 + 5, w["b_w_o"], j, nb, t)
        x = _ffn_call(x, mods, r0 + 6, w["norm_g"], l * 3 + 2, w["ffn_w_gu"], w["ffn_w_d"], l, 1, nb, t,
                      final_g=w["final_g"] if l == DEPTH - 1 else None)
    k_new = k_new.reshape(bsz, s, N_HEADS, 2, HEAD_DIM)
    v_new = v_new.reshape(bsz, s, N_HEADS, V_DIM)
    return x, jnp.stack(convs, 0), jnp.stack(hs, 0), k_new, v_new


def kernel(x_prompt, x_sample, c_prompt, c_sample, state_conv, state_h, cache_k, cache_v, page_table, ada_w, ada_b, norm_g, ffn_w_gu, ffn_w_d, a_w_in, a_conv_w, a_conv_b, a_w_rg, a_b_rg, a_w_ig, a_b_ig, a_lambda, a_w_out, kv_ada_w, kv_ada_b, kv_norm_g, w_kv, b_w_q, b_lambda, b_subln_g, b_w_o, final_g):
    bp, sp, _ = x_prompt.shape
    bs, ss, _ = x_sample.shape
    assert ss == SUBLANES and sp % PROMPT_TILE == 0 and PROMPT_TILE == ATTN_TILE

    w = {
        "norm_g": norm_g.reshape(DEPTH * 3, 1, D_MODEL),
        "ffn_w_gu": ffn_w_gu.astype(BF16),
        "ffn_w_d": ffn_w_d.astype(BF16),
        "a_w_in": a_w_in.astype(BF16),
        "a_conv_w": a_conv_w,
        "a_conv_b": a_conv_b.reshape(N_A_LAYERS, 1, D_RNN),
        "a_w_gate": jnp.concatenate([a_w_rg, a_w_ig], axis=-1).astype(BF16),
        "a_b_rg": a_b_rg.reshape(N_A_LAYERS, 1, D_RNN),
        "a_b_ig": a_b_ig.reshape(N_A_LAYERS, 1, D_RNN),
        "a_lambda": a_lambda.reshape(N_A_LAYERS, 1, D_RNN),
        "a_w_out": a_w_out.astype(BF16),
        "kv_norm_g": kv_norm_g.reshape(1, D_MODEL),
        "w_kv": w_kv.astype(BF16),
        "b_w_q": b_w_q.astype(BF16),
        "b_w_o": b_w_o.astype(BF16),
        "final_g": final_g.reshape(1, D_MODEL),
    }
    subln_g = b_subln_g.reshape(-1, 1, V_DIM)

    pad = jnp.zeros((C_ROWS_SAMPLE - bp, D_MODEL), F32)
    c_all = jnp.concatenate([c_prompt, pad, c_sample], axis=0)
    rows = _ada_call(c_all, ada_w, ada_b.reshape(DEPTH, 1, -1), tn=N_ADA * D_MODEL // 4)
    kvrows = _ada_call(c_all, kv_ada_w[None], kv_ada_b.reshape(1, 1, -1), tn=2 * D_MODEL)

    def prompt_attn(x, mods, row0, g_row, j, lam_init, cos, sin, kb, vb):
        q = _q_call(x, mods, row0, w["norm_g"], g_row, w["b_w_q"], j, cos, sin, 1, PROMPT_TILE, BF16)
        return _attn_prompt_call(q, kb, vb, b_lambda, subln_g, j, lam_init, ATTN_TILE)

    y_p, conv_p, h_p, k_p, v_p = _run_trunk(
        x_prompt, _group_rows(rows, C_ROWS_PROMPT, bp), _group_rows(kvrows, C_ROWS_PROMPT, bp),
        jnp.zeros((N_A_LAYERS, bp, SUBLANES, D_RNN), F32), jnp.zeros((N_A_LAYERS, bp, 1, D_RNN), F32),
        0, w, 1, PROMPT_TILE, prompt_attn)

    n_phys = cache_k.shape[0]
    ck = cache_k.reshape(n_phys, PAGE_SIZE, D_MODEL)
    cv = cache_v.reshape(n_phys, PAGE_SIZE, D_MODEL)
    past_len = page_table.shape[1] * PAGE_SIZE

    def sample_attn(x, mods, row0, g_row, j, lam_init, cos, sin, kb, vb):
        q = _q_call(x, mods, row0, w["norm_g"], g_row, w["b_w_q"], j, cos, sin, bs, ss, F32)
        padn = ((0, 0), (0, PAGE_SIZE - ss), (0, 0))
        return _attn_decode_call(q, ck, cv, page_table, jnp.pad(kb, padn), jnp.pad(vb, padn),
                                 b_lambda, subln_g, j, lam_init, PAGES_PER_STEP)

    conv0 = jnp.pad(state_conv, ((0, 0), (0, 0), (SUBLANES - (CONV_WIDTH - 1), 0), (0, 0)))
    y_s, conv_s, h_s, k_s, v_s = _run_trunk(
        x_sample, _group_rows(rows, C_ROWS_SAMPLE, bs), _group_rows(kvrows, C_ROWS_SAMPLE, bs),
        conv0, state_h[:, :, None, :], past_len, w, bs, ss, sample_attn)

    return (y_p, y_s, conv_p, h_p, k_p, v_p, conv_s, h_s, k_s, v_s)
```

```python
import functools
import math

import jax
import jax.numpy as jnp
from jax import lax
from jax.experimental import pallas as pl
from jax.experimental.pallas import tpu as pltpu

D_MODEL = 1024
DEPTH = 4
N_A_LAYERS = DEPTH // 2
D_RNN = D_MODEL
N_LRU_BLOCKS = 8
LRU_BLOCK = D_RNN // N_LRU_BLOCKS
CONV_WIDTH = 4
LRU_C = 8.0
N_HEADS = 8
HEAD_DIM = D_MODEL // (2 * N_HEADS)
V_DIM = 2 * HEAD_DIM
D_FF = 2816
ROPE_THETA = 10000.0
NORM_EPS = 1e-6
N_ADA = 9
PAGE_SIZE = 128

SUBLANES = 8
LANES = 128
VMEM_LIMIT_BYTES = 56 * 1024 * 1024

LOG2_E = math.log2(math.e)

F32 = jnp.float32
BF16 = jnp.bfloat16


def _params(*sem):
    return pltpu.CompilerParams(dimension_semantics=sem, vmem_limit_bytes=VMEM_LIMIT_BYTES)


def _const_spec(shape, index_map):
    return pl.BlockSpec(shape, index_map, pipeline_mode=pl.Buffered(1))


def _modulate(x, g, shift, scale):
    ms = jnp.mean(x * x, axis=-1, keepdims=True)
    y = x * lax.rsqrt(ms + NORM_EPS)
    return (y * g) * (1.0 + scale) + shift


def _ada_kernel(c_ref, w_ref, b_ref, o_ref):
    c_act = jax.nn.silu(c_ref[...]).astype(BF16)
    o_ref[...] = jnp.dot(c_act, w_ref[...].astype(BF16), preferred_element_type=F32) + b_ref[...]


def _ada_call(c_all, w, b, tn):
    nl, _, n = w.shape
    r = c_all.shape[0]
    return pl.pallas_call(
        _ada_kernel,
        grid=(nl, n // tn),
        in_specs=[
            pl.BlockSpec((r, D_MODEL), lambda l, j: (0, 0)),
            pl.BlockSpec((None, D_MODEL, tn), lambda l, j: (l, 0, j)),
            pl.BlockSpec((None, 1, tn), lambda l, j: (l, 0, j)),
        ],
        out_specs=pl.BlockSpec((None, r, tn), lambda l, j: (l, 0, j)),
        out_shape=jax.ShapeDtypeStruct((nl, r, n), F32),
        compiler_params=_params("arbitrary", "arbitrary"),
        name="ada_rows",
    )(c_all, w, b)


def _ffn_kernel(x_ref, sh_ref, sc_ref, gt_ref, g_ref, wgu_ref, wd_ref, *rest, final):
    if final:
        fg_ref, o_ref = rest
    else:
        (o_ref,) = rest
    x = x_ref[...]
    nb, t, _ = x.shape
    h = _modulate(x, g_ref[...], sh_ref[...], sc_ref[...])
    hb = h.reshape(nb * t, D_MODEL).astype(BF16)
    gu = jnp.dot(hb, wgu_ref[...], preferred_element_type=F32)
    a = (jax.nn.silu(gu[:, :D_FF]) * gu[:, D_FF:]).astype(BF16)
    y = jnp.dot(a, wd_ref[...], preferred_element_type=F32).reshape(nb, t, D_MODEL)
    out = x + (0.5 * gt_ref[...]) * y
    if final:
        ms = jnp.mean(out * out, axis=-1, keepdims=True)
        out = (out * lax.rsqrt(ms + NORM_EPS)) * fg_ref[...]
    o_ref[...] = out


def _mod_spec(row, nb):
    return pl.BlockSpec((None, nb, 1, D_MODEL), lambda b, t: (row, b, 0, 0))


def _tok_spec(nb, t, width=D_MODEL):
    return pl.BlockSpec((nb, t, width), lambda b, i: (b, i, 0))


def _ffn_call(x, mods, row0, norm_g3, g_row, wgu, wd, l, j, nb, t, final_g=None):
    bsz, s, _ = x.shape
    final = final_g is not None
    in_specs = [
        _tok_spec(nb, t),
        _mod_spec(row0, nb), _mod_spec(row0 + 1, nb), _mod_spec(row0 + 2, nb),
        pl.BlockSpec((None, 1, D_MODEL), lambda b, i: (g_row, 0, 0)),
        _const_spec((None, None, D_MODEL, 2 * D_FF), lambda b, i: (l, j, 0, 0)),
        _const_spec((None, None, D_FF, D_MODEL), lambda b, i: (l, j, 0, 0)),
    ]
    args = [x, mods, mods, mods, norm_g3, wgu, wd]
    if final:
        in_specs.append(pl.BlockSpec((1, D_MODEL), lambda b, i: (0, 0)))
        args.append(final_g)
    return pl.pallas_call(
        functools.partial(_ffn_kernel, final=final),
        grid=(bsz // nb, s // t),
        in_specs=in_specs,
        out_specs=_tok_spec(nb, t),
        out_shape=jax.ShapeDtypeStruct(x.shape, F32),
        compiler_params=_params("arbitrary", "arbitrary"),
        name="ffn",
    )(*args)


def _shift_rows(x, tail, j, row8):
    t = x.shape[1]
    r = pltpu.roll(x, j, axis=1)
    head = jnp.where(row8 < j, pltpu.roll(tail, j, axis=1), r[:, :SUBLANES])
    if t == SUBLANES:
        return head
    return jnp.concatenate([head, r[:, SUBLANES:]], axis=1)


def _rglru_kernel(x_ref, sh_ref, sc_ref, gt_ref, g_ref, conv0_ref, h0_ref,
                  win_ref, cw_ref, cb_ref, wg_ref, brg_ref, big_ref, lam_ref, wout_ref,
                  xo_ref, convo_ref, ho_ref,
                  tail_sc, h_sc, a_sc, u_sc, hs_sc):
    @pl.when(pl.program_id(1) == 0)
    def _():
        tail_sc[...] = conv0_ref[...]
        h_sc[...] = h0_ref[...]

    x = x_ref[...]
    nb, t, _ = x.shape
    m = nb * t
    h = _modulate(x, g_ref[...], sh_ref[...], sc_ref[...])
    xy = jnp.dot(h.reshape(m, D_MODEL).astype(BF16), win_ref[...], preferred_element_type=F32)
    xb = xy[:, :D_RNN].reshape(nb, t, D_RNN)
    yb = jax.nn.gelu(xy[:, D_RNN:])

    tail = tail_sc[...]
    row8 = lax.broadcasted_iota(jnp.int32, (1, SUBLANES, 1), 1)
    cw = cw_ref[...]
    xc = cb_ref[...] + cw[0:1] * _shift_rows(xb, tail, 3, row8)
    xc = xc + cw[1:2] * _shift_rows(xb, tail, 2, row8)
    xc = xc + cw[2:3] * _shift_rows(xb, tail, 1, row8)
    xc = xc + cw[3:4] * xb
    new_tail = xb[:, t - SUBLANES:, :]
    tail_sc[...] = new_tail
    convo_ref[...] = new_tail

    xc2 = xc.reshape(m, D_RNN)
    xcb = xc2.astype(BF16)
    pre = [jnp.dot(xcb[:, n * LRU_BLOCK:(n + 1) * LRU_BLOCK], wg_ref[n], preferred_element_type=F32)
           for n in range(N_LRU_BLOCKS)]
    rpre = jnp.concatenate([p[:, :LRU_BLOCK] for p in pre], axis=1)
    ipre = jnp.concatenate([p[:, LRU_BLOCK:] for p in pre], axis=1)
    r = jax.nn.sigmoid(rpre + brg_ref[...])
    i = jax.nn.sigmoid(ipre + big_ref[...])
    log_a = (-LRU_C * r) * jax.nn.softplus(-lam_ref[...])
    a_sc[...] = jnp.exp(log_a).reshape(nb, t, D_RNN)
    th = jnp.tanh(log_a)
    one_minus_a2 = (-2.0 * th) / (1.0 - th)
    u_sc[...] = (jnp.sqrt(one_minus_a2) * (i * xc2)).reshape(nb, t, D_RNN)

    def group(gi, hprev):
        s0 = pl.multiple_of(gi * SUBLANES, SUBLANES)
        ca = a_sc[:, pl.ds(s0, SUBLANES), :]
        cb = u_sc[:, pl.ds(s0, SUBLANES), :]
        for s in (1, 2, 4):
            keep = row8 >= s
            cb = jnp.where(keep, ca * pltpu.roll(cb, s, axis=1) + cb, cb)
            ca = jnp.where(keep, ca * pltpu.roll(ca, s, axis=1), ca)
        hs = ca * hprev + cb
        hs_sc[:, pl.ds(s0, SUBLANES), :] = hs
        return hs[:, SUBLANES - 1:SUBLANES, :]

    hlast = lax.fori_loop(0, t // SUBLANES, group, h_sc[...])
    h_sc[...] = hlast
    ho_ref[...] = hlast

    mixed = (hs_sc[...].reshape(m, D_RNN) * yb).astype(BF16)
    out = jnp.dot(mixed, wout_ref[...], preferred_element_type=F32).reshape(nb, t, D_MODEL)
    xo_ref[...] = x + gt_ref[...] * out


def _rglru_call(x, mods, row0, norm_g3, g_row, conv0, h0, w_in, conv_w, conv_b, w_gate, b_rg, b_ig,
                lam, w_out, l, nb, t):
    bsz, s, _ = x.shape
    vec = lambda: pl.BlockSpec((None, 1, D_RNN), lambda b, i: (l, 0, 0))
    return pl.pallas_call(
        _rglru_kernel,
        grid=(bsz // nb, s // t),
        in_specs=[
            _tok_spec(nb, t),
            _mod_spec(row0, nb), _mod_spec(row0 + 1, nb), _mod_spec(row0 + 2, nb),
            pl.BlockSpec((None, 1, D_MODEL), lambda b, i: (g_row, 0, 0)),
            pl.BlockSpec((None, nb, SUBLANES, D_RNN), lambda b, i: (l, b, 0, 0)),
            pl.BlockSpec((None, nb, 1, D_RNN), lambda b, i: (l, b, 0, 0)),
            _const_spec((None, D_MODEL, 2 * D_RNN), lambda b, i: (l, 0, 0)),
            pl.BlockSpec((None, CONV_WIDTH, D_RNN), lambda b, i: (l, 0, 0)),
            vec(),
            _const_spec((None, N_LRU_BLOCKS, LRU_BLOCK, 2 * LRU_BLOCK), lambda b, i: (l, 0, 0, 0)),
            vec(), vec(), vec(),
            _const_spec((None, D_RNN, D_MODEL), lambda b, i: (l, 0, 0)),
        ],
        out_specs=[
            _tok_spec(nb, t),
            pl.BlockSpec((nb, SUBLANES, D_RNN), lambda b, i: (b, 0, 0)),
            pl.BlockSpec((nb, 1, D_RNN), lambda b, i: (b, 0, 0)),
        ],
        out_shape=[
            jax.ShapeDtypeStruct(x.shape, F32),
            jax.ShapeDtypeStruct((bsz, SUBLANES, D_RNN), F32),
            jax.ShapeDtypeStruct((bsz, 1, D_RNN), F32),
        ],
        scratch_shapes=[
            pltpu.VMEM((nb, SUBLANES, D_RNN), F32),
            pltpu.VMEM((nb, 1, D_RNN), F32),
            pltpu.VMEM((nb, t, D_RNN), F32),
            pltpu.VMEM((nb, t, D_RNN), F32),
            pltpu.VMEM((nb, t, D_RNN), F32),
        ],
        compiler_params=_params("arbitrary", "arbitrary"),
        name="rglru",
    )(x, mods, mods, mods, norm_g3, conv0, h0, w_in, conv_w, conv_b, w_gate, b_rg, b_ig, lam, w_out)


def _rope(x, cos_ref, sin_ref):
    reps = D_MODEL // LANES
    cos = jnp.concatenate([cos_ref[...]] * reps, axis=-1)[None]
    sin = jnp.concatenate([sin_ref[...]] * reps, axis=-1)[None]
    lane = lax.broadcasted_iota(jnp.int32, (1, 1, D_MODEL), 2)
    low_half = (lane % HEAD_DIM) < (HEAD_DIM // 2)
    swapped = jnp.where(low_half,
                        pltpu.roll(x, D_MODEL - HEAD_DIM // 2, axis=2),
                        pltpu.roll(x, HEAD_DIM // 2, axis=2))
    return x * cos + swapped * sin


def _kv_kernel(x_ref, sh_ref, sc_ref, g_ref, w_ref, cos_ref, sin_ref, k_ref, v_ref, kb_ref, vb_ref):
    x = x_ref[...]
    nb, t, _ = x.shape
    h = _modulate(x, g_ref[...], sh_ref[...], sc_ref[...])
    y = jnp.dot(h.reshape(nb * t, D_MODEL).astype(BF16), w_ref[...], preferred_element_type=F32)
    k = _rope(y[:, :D_MODEL].reshape(nb, t, D_MODEL), cos_ref, sin_ref)
    v = y[:, D_MODEL:].reshape(nb, t, D_MODEL)
    k_ref[...] = k
    v_ref[...] = v
    kb_ref[...] = k.astype(BF16)
    vb_ref[...] = v.astype(BF16)


def _q_kernel(x_ref, sh_ref, sc_ref, g_ref, w_ref, cos_ref, sin_ref, q_ref):
    x = x_ref[...]
    nb, t, _ = x.shape
    h = _modulate(x, g_ref[...], sh_ref[...], sc_ref[...])
    y = jnp.dot(h.reshape(nb * t, D_MODEL).astype(BF16), w_ref[...], preferred_element_type=F32)
    q = _rope(y.reshape(nb, t, D_MODEL), cos_ref, sin_ref) * (HEAD_DIM ** -0.5 * LOG2_E)
    q_ref[...] = q.astype(q_ref.dtype)


def _proj_specs(mods, row0, g_spec, w_spec, nb, t):
    rope_spec = pl.BlockSpec((t, LANES), lambda b, i: (i, 0))
    return [_tok_spec(nb, t), _mod_spec(row0, nb), _mod_spec(row0 + 1, nb), g_spec, w_spec, rope_spec, rope_spec]


def _kv_call(x, kvmods, kv_norm_g, w_kv, cos, sin, nb, t):
    bsz, s, _ = x.shape
    g_spec = pl.BlockSpec((1, D_MODEL), lambda b, i: (0, 0))
    w_spec = _const_spec((D_MODEL, 2 * D_MODEL), lambda b, i: (0, 0))
    shp = lambda dt: jax.ShapeDtypeStruct(x.shape, dt)
    return pl.pallas_call(
        _kv_kernel,
        grid=(bsz // nb, s // t),
        in_specs=_proj_specs(kvmods, 0, g_spec, w_spec, nb, t),
        out_specs=[_tok_spec(nb, t)] * 4,
        out_shape=[shp(F32), shp(F32), shp(BF16), shp(BF16)],
        compiler_params=_params("arbitrary", "arbitrary"),
        name="kv_proj",
    )(x, kvmods, kvmods, kv_norm_g, w_kv, cos, sin)


def _q_call(x, mods, row0, norm_g3, g_row, w_q, j, cos, sin, nb, t, out_dtype):
    bsz, s, _ = x.shape
    g_spec = pl.BlockSpec((None, 1, D_MODEL), lambda b, i: (g_row, 0, 0))
    w_spec = _const_spec((None, D_MODEL, D_MODEL), lambda b, i: (j, 0, 0))
    return pl.pallas_call(
        _q_kernel,
        grid=(bsz // nb, s // t),
        in_specs=_proj_specs(mods, row0, g_spec, w_spec, nb, t),
        out_specs=_tok_spec(nb, t),
        out_shape=jax.ShapeDtypeStruct(x.shape, out_dtype),
        compiler_params=_params("arbitrary", "arbitrary"),
        name="q_proj",
    )(x, mods, mods, norm_g3, w_q, cos, sin)


def _diff_lambda(lamp_ref, lam_init):
    lp = lamp_ref[...]
    e1 = jnp.exp(jnp.sum(lp[0:1] * lp[1:2], axis=-1, keepdims=True))
    e2 = jnp.exp(jnp.sum(lp[2:3] * lp[3:4], axis=-1, keepdims=True))
    return e1 - e2 + lam_init


def _subln(d, sg, lam_init):
    ms = jnp.mean(d * d, axis=-1, keepdims=True)
    return ((d * lax.rsqrt(ms + NORM_EPS)) * sg) * (1.0 - lam_init)


def _attn_prompt_kernel(q_ref, k_ref, v_ref, lamp_ref, sg_ref, o_ref, m_sc, acc_sc, *, lam_init, tq):
    qi = pl.program_id(2)
    q = q_ref[...]
    lane = lax.broadcasted_iota(jnp.int32, (1, V_DIM), 1)
    zero = jnp.zeros_like(q)
    qq = jnp.concatenate([jnp.where(lane < HEAD_DIM, q, zero), jnp.where(lane >= HEAD_DIM, q, zero)], axis=0)
    m_sc[...] = jnp.full(m_sc.shape, -jnp.inf, F32)
    acc_sc[...] = jnp.zeros(acc_sc.shape, F32)
    ones = jnp.ones((tq, V_DIM), BF16)

    def step(off, masked):
        kb = k_ref[pl.ds(off, tq), :]
        vb = jnp.concatenate([v_ref[pl.ds(off, tq), :], ones], axis=1)
        if masked:
            r = lax.broadcasted_iota(jnp.int32, (tq, tq), 0)
            c = lax.broadcasted_iota(jnp.int32, (tq, tq), 1)
            ok = c <= r
        halves = [slice(mp * tq, (mp + 1) * tq) for mp in range(2)]
        scores = [lax.dot_general(qq[rows], kb, (((1,), (1,)), ((), ())), preferred_element_type=F32)
                  for rows in halves]
        for rows, s in zip(halves, scores):
            if masked:
                s = jnp.where(ok, s, -jnp.inf)
            m_prev = m_sc[rows]
            m_new = jnp.maximum(m_prev, jnp.max(s, axis=-1, keepdims=True))
            alpha = jnp.exp2(m_prev - m_new)
            p = jnp.exp2(s - jnp.concatenate([m_new] * (tq // LANES), axis=1))
            pv = jnp.dot(p.astype(BF16), vb, preferred_element_type=F32)
            acc_sc[rows] = jnp.concatenate([alpha, alpha], axis=1) * acc_sc[rows] + pv
            m_sc[rows] = m_new

    def body(j, carry):
        step(pl.multiple_of(2 * j * tq, tq), False)
        step(pl.multiple_of((2 * j + 1) * tq, tq), False)
        return carry

    lax.fori_loop(0, qi // 2, body, 0)

    @pl.when(qi % 2 == 1)
    def _():
        step(pl.multiple_of((qi - 1) * tq, tq), False)

    step(pl.multiple_of(qi * tq, tq), True)

    acc = acc_sc[...]
    o = acc[:, :V_DIM] / acc[:, V_DIM:]
    lam = _diff_lambda(lamp_ref, lam_init)
    d = o[:tq] - lam * o[tq:]
    o_ref[...] = _subln(d, sg_ref[...], lam_init).astype(o_ref.dtype)


def _attn_prompt_call(q, k, v, lam_p, subln_g, j, lam_init, tq):
    bsz, s, _ = q.shape
    return pl.pallas_call(
        functools.partial(_attn_prompt_kernel, lam_init=lam_init, tq=tq),
        grid=(bsz, N_HEADS, s // tq),
        in_specs=[
            pl.BlockSpec((None, tq, V_DIM), lambda b, h, i: (b, i, h)),
            pl.BlockSpec((None, s, V_DIM), lambda b, h, i: (b, 0, h)),
            pl.BlockSpec((None, s, V_DIM), lambda b, h, i: (b, 0, h)),
            pl.BlockSpec((None, 4, HEAD_DIM), lambda b, h, i: (j, 0, 0)),
            pl.BlockSpec((None, 1, V_DIM), lambda b, h, i: (j, 0, 0)),
        ],
        out_specs=pl.BlockSpec((None, tq, V_DIM), lambda b, h, i: (b, i, h)),
        out_shape=jax.ShapeDtypeStruct(q.shape, BF16),
        scratch_shapes=[
            pltpu.VMEM((2 * tq, LANES), F32),
            pltpu.VMEM((2 * tq, 2 * V_DIM), F32),
        ],
        compiler_params=_params("arbitrary", "arbitrary", "arbitrary"),
        name="attn_prompt",
    )(q, k, v, lam_p, subln_g)


def _attn_decode_kernel(pt_ref, q_ref, *refs, pps, nstep, lam_init):
    k_refs = refs[:pps]
    v_refs = refs[pps:2 * pps]
    kn_ref, vn_ref, lamp_ref, sg_ref, o_ref, m_sc, l_sc, acc_sc = refs[2 * pps:]
    step_id = pl.program_id(1)
    nrow = 2 * N_HEADS * SUBLANES

    @pl.when(step_id == 0)
    def _():
        m_sc[...] = jnp.full(m_sc.shape, -jnp.inf, F32)
        l_sc[...] = jnp.zeros(l_sc.shape, F32)
        acc_sc[...] = jnp.zeros(acc_sc.shape, F32)

    q = q_ref[...]
    row = lax.broadcasted_iota(jnp.int32, (nrow, D_MODEL), 0)
    col = lax.broadcasted_iota(jnp.int32, (nrow, D_MODEL), 1)
    wt = jnp.where((row // SUBLANES) == (col // HEAD_DIM),
                   jnp.concatenate([q] * (nrow // SUBLANES), axis=0), 0.0).astype(BF16)
    hrows = 2 * SUBLANES

    def update(s, v_of):
        m_prev = m_sc[...]
        m_new = jnp.maximum(m_prev, jnp.max(s, axis=-1, keepdims=True))
        alpha = jnp.exp2(m_prev - m_new)
        n_pg = s.shape[1] // PAGE_SIZE
        p = jnp.exp2(s - jnp.concatenate([m_new] * n_pg, axis=1))
        l_sc[...] = alpha * l_sc[...] + jnp.sum(p, axis=-1, keepdims=True)
        p = p.astype(BF16)
        heads = []
        for hd in range(N_HEADS):
            ph = p[hd * hrows:(hd + 1) * hrows]
            pv = jnp.dot(ph[:, :PAGE_SIZE], v_of(0, hd), preferred_element_type=F32)
            for i in range(1, n_pg):
                pv = pv + jnp.dot(ph[:, i * PAGE_SIZE:(i + 1) * PAGE_SIZE], v_of(i, hd),
                                  preferred_element_type=F32)
            heads.append(pv)
        acc_sc[...] = alpha * acc_sc[...] + jnp.concatenate(heads, axis=0)
        m_sc[...] = m_new

    @pl.when(step_id < nstep)
    def _():
        ss = [jnp.dot(wt, k_refs[i][...].astype(BF16), preferred_element_type=F32) for i in range(pps)]
        update(jnp.concatenate(ss, axis=1),
               lambda i, hd: v_refs[i][pl.ds(hd, PAGE_SIZE, stride=N_HEADS), :].astype(BF16))

    @pl.when(step_id == nstep)
    def _():
        s = lax.dot_general(wt, kn_ref[...], (((1,), (1,)), ((), ())), preferred_element_type=F32)
        r = lax.broadcasted_iota(jnp.int32, (nrow, PAGE_SIZE), 0)
        c = lax.broadcasted_iota(jnp.int32, (nrow, PAGE_SIZE), 1)
        update(jnp.where(c <= (r % SUBLANES), s, -jnp.inf),
               lambda i, hd: vn_ref[:, hd * V_DIM:(hd + 1) * V_DIM])
        o = acc_sc[...] / l_sc[...]
        lam = _diff_lambda(lamp_ref, lam_init)
        heads = []
        for hd in range(N_HEADS):
            r0 = hd * hrows
            d = o[r0:r0 + SUBLANES] - lam * o[r0 + SUBLANES:r0 + hrows]
            heads.append(_subln(d, sg_ref[...], lam_init))
        o_ref[...] = jnp.concatenate(heads, axis=1)


def _attn_decode_call(q, cache_kt, cache_v, page_table, kn_pad, vn_pad, lam_p, subln_g, j, lam_init, pps):
    bsz = q.shape[0]
    n_pages = page_table.shape[1]
    nstep = n_pages // pps

    def page_spec(i, rows, cols):
        return pl.BlockSpec(
            (None, rows, cols),
            lambda b, s, pt: (pt[b, jnp.minimum(s * pps + i, n_pages - 1)], 0, 0))

    tok = pl.BlockSpec((None, SUBLANES, D_MODEL), lambda b, s, pt: (b, 0, 0))
    new = pl.BlockSpec((None, PAGE_SIZE, D_MODEL), lambda b, s, pt: (b, 0, 0))
    nrow = 2 * N_HEADS * SUBLANES
    grid_spec = pltpu.PrefetchScalarGridSpec(
        num_scalar_prefetch=1,
        grid=(bsz, nstep + 1),
        in_specs=[tok] + [page_spec(i, D_MODEL, PAGE_SIZE) for i in range(pps)]
        + [page_spec(i, PAGE_SIZE * N_HEADS, V_DIM) for i in range(pps)] + [
            new, new,
            pl.BlockSpec((None, 4, HEAD_DIM), lambda b, s, pt: (j, 0, 0)),
            pl.BlockSpec((None, 1, V_DIM), lambda b, s, pt: (j, 0, 0)),
        ],
        out_specs=tok,
        scratch_shapes=[
            pltpu.VMEM((nrow, LANES), F32),
            pltpu.VMEM((nrow, LANES), F32),
            pltpu.VMEM((nrow, V_DIM), F32),
        ],
    )
    return pl.pallas_call(
        functools.partial(_attn_decode_kernel, pps=pps, nstep=nstep, lam_init=lam_init),
        grid_spec=grid_spec,
        out_shape=jax.ShapeDtypeStruct(q.shape, F32),
        compiler_params=_params("arbitrary", "arbitrary"),
        name="attn_decode",
    )(page_table, q, *([cache_kt] * pps), *([cache_v] * pps), kn_pad, vn_pad, lam_p, subln_g)


def _oproj_kernel(x_ref, o_ref, gt_ref, w_ref, xo_ref):
    x = x_ref[...]
    nb, t, _ = x.shape
    y = jnp.dot(o_ref[...].reshape(nb * t, D_MODEL).astype(BF16), w_ref[...], preferred_element_type=F32)
    xo_ref[...] = x + gt_ref[...] * y.reshape(nb, t, D_MODEL)


def _oproj_call(x, o, mods, row, w_o, j, nb, t):
    bsz, s, _ = x.shape
    return pl.pallas_call(
        _oproj_kernel,
        grid=(bsz // nb, s // t),
        in_specs=[
            _tok_spec(nb, t), _tok_spec(nb, t), _mod_spec(row, nb),
            _const_spec((None, D_MODEL, D_MODEL), lambda b, i: (j, 0, 0)),
        ],
        out_specs=_tok_spec(nb, t),
        out_shape=jax.ShapeDtypeStruct(x.shape, F32),
        compiler_params=_params("arbitrary", "arbitrary"),
        name="o_proj",
    )(x, o, mods, w_o)


PROMPT_TILE = 512
ATTN_TILE = 512
PAGES_PER_STEP = 8
C_ROWS_PROMPT = 0
C_ROWS_SAMPLE = 8


def _rope_tables(pos):
    half = HEAD_DIM // 2
    inv = 1.0 / (ROPE_THETA ** (jnp.arange(half, dtype=F32) * (2.0 / HEAD_DIM)))
    ang = pos.astype(F32)[:, None] * inv[None, :]
    cos = jnp.tile(jnp.cos(ang), (1, LANES // half))
    sign = jnp.where((jnp.arange(LANES) % HEAD_DIM) < half, -1.0, 1.0).astype(F32)
    sin = jnp.tile(jnp.sin(ang), (1, LANES // half)) * sign[None, :]
    return cos, sin


def _group_rows(rows, lo, n):
    nl, _, kd = rows.shape
    k = kd // D_MODEL
    g = rows[:, lo:lo + n].reshape(nl, n, k, D_MODEL)
    return jnp.transpose(g, (0, 2, 1, 3)).reshape(nl * k, n, 1, D_MODEL)


def _run_trunk(x, mods, kvmods, conv0, h0, past_len, w, nb, t, attn_fn):
    bsz, s, _ = x.shape
    pos = past_len + jnp.arange(s, dtype=jnp.int32)
    cos, sin = _rope_tables(pos)
    convs, hs = [], []
    k_new = v_new = kb = vb = None
    for l in range(DEPTH):
        if l == N_A_LAYERS:
            k_new, v_new, kb, vb = _kv_call(x, kvmods, w["kv_norm_g"], w["w_kv"], cos, sin, nb, t)
        r0 = l * N_ADA
        x = _ffn_call(x, mods, r0, w["norm_g"], l * 3, w["ffn_w_gu"], w["ffn_w_d"], l, 0, nb, t)
        if l < N_A_LAYERS:
            x, cs, hl = _rglru_call(x, mods, r0 + 3, w["norm_g"], l * 3 + 1, conv0, h0,
                                    w["a_w_in"], w["a_conv_w"], w["a_conv_b"], w["a_w_gate"],
                                    w["a_b_rg"], w["a_b_ig"], w["a_lambda"], w["a_w_out"], l, nb, t)
            convs.append(cs[:, SUBLANES - (CONV_WIDTH - 1):])
            hs.append(hl[:, 0])
        else:
            j = l - N_A_LAYERS
            lam_init = 0.8 - 0.6 * math.exp(-0.3 * l)
            o = attn_fn(x, mods, r0 + 3, l * 3 + 1, j, lam_init, cos, sin, kb, vb)
            x = _oproj_call(x, o, mods, r0 + 5, w["b_w_o"], j, nb, t)
        x = _ffn_call(x, mods, r0 + 6, w["norm_g"], l * 3 + 2, w["ffn_w_gu"], w["ffn_w_d"], l, 1, nb, t,
                      final_g=w["final_g"] if l == DEPTH - 1 else None)
    k_new = k_new.reshape(bsz, s, N_HEADS, 2, HEAD_DIM)
    v_new = v_new.reshape(bsz, s, N_HEADS, V_DIM)
    return x, jnp.stack(convs, 0), jnp.stack(hs, 0), k_new, v_new


def kernel(x_prompt, x_sample, c_prompt, c_sample, state_conv, state_h, cache_k, cache_v, page_table, ada_w, ada_b, norm_g, ffn_w_gu, ffn_w_d, a_w_in, a_conv_w, a_conv_b, a_w_rg, a_b_rg, a_w_ig, a_b_ig, a_lambda, a_w_out, kv_ada_w, kv_ada_b, kv_norm_g, w_kv, b_w_q, b_lambda, b_subln_g, b_w_o, final_g):
    bp, sp, _ = x_prompt.shape
    bs, ss, _ = x_sample.shape
    assert ss == SUBLANES and sp % PROMPT_TILE == 0 and PROMPT_TILE == ATTN_TILE

    w = {
        "norm_g": norm_g.reshape(DEPTH * 3, 1, D_MODEL),
        "ffn_w_gu": ffn_w_gu.astype(BF16),
        "ffn_w_d": ffn_w_d.astype(BF16),
        "a_w_in": a_w_in.astype(BF16),
        "a_conv_w": a_conv_w,
        "a_conv_b": a_conv_b.reshape(N_A_LAYERS, 1, D_RNN),
        "a_w_gate": jnp.concatenate([a_w_rg, a_w_ig], axis=-1).astype(BF16),
        "a_b_rg": a_b_rg.reshape(N_A_LAYERS, 1, D_RNN),
        "a_b_ig": a_b_ig.reshape(N_A_LAYERS, 1, D_RNN),
        "a_lambda": a_lambda.reshape(N_A_LAYERS, 1, D_RNN),
        "a_w_out": a_w_out.astype(BF16),
        "kv_norm_g": kv_norm_g.reshape(1, D_MODEL),
        "w_kv": w_kv.astype(BF16),
        "b_w_q": b_w_q.astype(BF16),
        "b_w_o": b_w_o.astype(BF16),
        "final_g": final_g.reshape(1, D_MODEL),
    }
    subln_g = b_subln_g.reshape(-1, 1, V_DIM)

    pad = jnp.zeros((C_ROWS_SAMPLE - bp, D_MODEL), F32)
    c_all = jnp.concatenate([c_prompt, pad, c_sample], axis=0)
    rows = _ada_call(c_all, ada_w, ada_b.reshape(DEPTH, 1, -1), tn=N_ADA * D_MODEL // 4)
    kvrows = _ada_call(c_all, kv_ada_w[None], kv_ada_b.reshape(1, 1, -1), tn=2 * D_MODEL)

    def prompt_attn(x, mods, row0, g_row, j, lam_init, cos, sin, kb, vb):
        q = _q_call(x, mods, row0, w["norm_g"], g_row, w["b_w_q"], j, cos, sin, 1, PROMPT_TILE, BF16)
        return _attn_prompt_call(q, kb, vb, b_lambda, subln_g, j, lam_init, ATTN_TILE)

    y_p, conv_p, h_p, k_p, v_p = _run_trunk(
        x_prompt, _group_rows(rows, C_ROWS_PROMPT, bp), _group_rows(kvrows, C_ROWS_PROMPT, bp),
        jnp.zeros((N_A_LAYERS, bp, SUBLANES, D_RNN), F32), jnp.zeros((N_A_LAYERS, bp, 1, D_RNN), F32),
        0, w, 1, PROMPT_TILE, prompt_attn)

    n_phys = cache_k.shape[0]
    ck = jnp.transpose(cache_k, (0, 2, 3, 4, 1)).reshape(n_phys, D_MODEL, PAGE_SIZE)
    cv = cache_v.reshape(n_phys, PAGE_SIZE * N_HEADS, V_DIM)
    past_len = page_table.shape[1] * PAGE_SIZE

    def sample_attn(x, mods, row0, g_row, j, lam_init, cos, sin, kb, vb):
        q = _q_call(x, mods, row0, w["norm_g"], g_row, w["b_w_q"], j, cos, sin, bs, ss, F32)
        padn = ((0, 0), (0, PAGE_SIZE - ss), (0, 0))
        return _attn_decode_call(q, ck, cv, page_table, jnp.pad(kb, padn), jnp.pad(vb, padn),
                                 b_lambda, subln_g, j, lam_init, PAGES_PER_STEP)

    conv0 = jnp.pad(state_conv, ((0, 0), (0, 0), (SUBLANES - (CONV_WIDTH - 1), 0), (0, 0)))
    y_s, conv_s, h_s, k_s, v_s = _run_trunk(
        x_sample, _group_rows(rows, C_ROWS_SAMPLE, bs), _group_rows(kvrows, C_ROWS_SAMPLE, bs),
        conv0, state_h[:, :, None, :], past_len, w, bs, ss, sample_attn)

    return (y_p, y_s, conv_p, h_p, k_p, v_p, conv_s, h_s, k_s, v_s)
```

```python
import functools
import math

import jax
import jax.numpy as jnp
from jax import lax
from jax.experimental import pallas as pl
from jax.experimental.pallas import tpu as pltpu

D_MODEL = 1024
DEPTH = 4
N_A_LAYERS = DEPTH // 2
D_RNN = D_MODEL
N_LRU_BLOCKS = 8
LRU_BLOCK = D_RNN // N_LRU_BLOCKS
CONV_WIDTH = 4
LRU_C = 8.0
N_HEADS = 8
HEAD_DIM = D_MODEL // (2 * N_HEADS)
V_DIM = 2 * HEAD_DIM
D_FF = 2816
ROPE_THETA = 10000.0
NORM_EPS = 1e-6
N_ADA = 9
PAGE_SIZE = 128

SUBLANES = 8
LANES = 128
VMEM_LIMIT_BYTES = 56 * 1024 * 1024

LOG2_E = math.log2(math.e)

F32 = jnp.float32
BF16 = jnp.bfloat16


def _params(*sem):
    return pltpu.CompilerParams(dimension_semantics=sem, vmem_limit_bytes=VMEM_LIMIT_BYTES)


def _const_spec(shape, index_map):
    return pl.BlockSpec(shape, index_map, pipeline_mode=pl.Buffered(1))


def _modulate(x, g, shift, scale):
    ms = jnp.mean(x * x, axis=-1, keepdims=True)
    y = x * lax.rsqrt(ms + NORM_EPS)
    return (y * g) * (1.0 + scale) + shift


def _ada_kernel(c_ref, w_ref, b_ref, o_ref):
    c_act = jax.nn.silu(c_ref[...]).astype(BF16)
    o_ref[...] = jnp.dot(c_act, w_ref[...].astype(BF16), preferred_element_type=F32) + b_ref[...]


def _ada_call(c_all, w, b, tn):
    nl, _, n = w.shape
    r = c_all.shape[0]
    return pl.pallas_call(
        _ada_kernel,
        grid=(nl, n // tn),
        in_specs=[
            pl.BlockSpec((r, D_MODEL), lambda l, j: (0, 0)),
            pl.BlockSpec((None, D_MODEL, tn), lambda l, j: (l, 0, j)),
            pl.BlockSpec((None, 1, tn), lambda l, j: (l, 0, j)),
        ],
        out_specs=pl.BlockSpec((None, r, tn), lambda l, j: (l, 0, j)),
        out_shape=jax.ShapeDtypeStruct((nl, r, n), F32),
        compiler_params=_params("arbitrary", "arbitrary"),
        name="ada_rows",
    )(c_all, w, b)


def _ffn_kernel(*refs, attn_out, final, query):
    it = iter(refs)
    x_ref = next(it)
    if attn_out:
        ao_ref, ogt_ref, wo_ref = next(it), next(it), next(it)
    sh_ref, sc_ref, gt_ref, g_ref, wgu_ref, wd_ref = (next(it) for _ in range(6))
    if final:
        fg_ref = next(it)
    if query:
        qsh_ref, qsc_ref, qg_ref, wq_ref, cos_ref, sin_ref = (next(it) for _ in range(6))
    o_ref = next(it)
    x = x_ref[...]
    nb, t, _ = x.shape
    m = nb * t
    if attn_out:
        ao = jnp.dot(ao_ref[...].reshape(m, D_MODEL).astype(BF16), wo_ref[...], preferred_element_type=F32)
        x = x + ogt_ref[...] * ao.reshape(nb, t, D_MODEL)
    h = _modulate(x, g_ref[...], sh_ref[...], sc_ref[...])
    hb = h.reshape(m, D_MODEL).astype(BF16)
    gu = jnp.dot(hb, wgu_ref[...], preferred_element_type=F32)
    a = (jax.nn.silu(gu[:, :D_FF]) * gu[:, D_FF:]).astype(BF16)
    y = jnp.dot(a, wd_ref[...], preferred_element_type=F32).reshape(nb, t, D_MODEL)
    out = x + (0.5 * gt_ref[...]) * y
    if final:
        ms = jnp.mean(out * out, axis=-1, keepdims=True)
        out = (out * lax.rsqrt(ms + NORM_EPS)) * fg_ref[...]
    o_ref[...] = out
    if query:
        q_ref = next(it)
        hq = _modulate(out, qg_ref[...], qsh_ref[...], qsc_ref[...])
        yq = jnp.dot(hq.reshape(m, D_MODEL).astype(BF16), wq_ref[...], preferred_element_type=F32)
        q = _rope(yq.reshape(nb, t, D_MODEL), cos_ref, sin_ref) * (HEAD_DIM ** -0.5 * LOG2_E)
        q_ref[...] = q.astype(q_ref.dtype)


def _mod_spec(row, nb):
    return pl.BlockSpec((None, nb, 1, D_MODEL), lambda b, t: (row, b, 0, 0))


def _tok_spec(nb, t, width=D_MODEL):
    return pl.BlockSpec((nb, t, width), lambda b, i: (b, i, 0))


def _ffn_call(x, mods, row0, norm_g3, g_row, wgu, wd, l, j, nb, t, final_g=None, attn_out=None, query=None):
    bsz, s, _ = x.shape
    in_specs = [_tok_spec(nb, t)]
    args = [x]
    if attn_out is not None:
        o, gate_row, w_o, oj = attn_out
        in_specs += [_tok_spec(nb, t), _mod_spec(gate_row, nb),
                     _const_spec((None, D_MODEL, D_MODEL), lambda b, i: (oj, 0, 0))]
        args += [o, mods, w_o]
    in_specs += [
        _mod_spec(row0, nb), _mod_spec(row0 + 1, nb), _mod_spec(row0 + 2, nb),
        pl.BlockSpec((None, 1, D_MODEL), lambda b, i: (g_row, 0, 0)),
        _const_spec((None, None, D_MODEL, 2 * D_FF), lambda b, i: (l, j, 0, 0)),
        _const_spec((None, None, D_FF, D_MODEL), lambda b, i: (l, j, 0, 0)),
    ]
    args += [mods, mods, mods, norm_g3, wgu, wd]
    if final_g is not None:
        in_specs.append(pl.BlockSpec((1, D_MODEL), lambda b, i: (0, 0)))
        args.append(final_g)
    out_specs = _tok_spec(nb, t)
    out_shape = jax.ShapeDtypeStruct(x.shape, F32)
    if query is not None:
        qrow0, qg_row, w_q, qj, cos, sin, qdtype = query
        rope_spec = pl.BlockSpec((t, LANES), lambda b, i: (i, 0))
        in_specs += [_mod_spec(qrow0, nb), _mod_spec(qrow0 + 1, nb),
                     pl.BlockSpec((None, 1, D_MODEL), lambda b, i: (qg_row, 0, 0)),
                     _const_spec((None, D_MODEL, D_MODEL), lambda b, i: (qj, 0, 0)),
                     rope_spec, rope_spec]
        args += [mods, mods, norm_g3, w_q, cos, sin]
        out_specs = [out_specs, _tok_spec(nb, t)]
        out_shape = [out_shape, jax.ShapeDtypeStruct(x.shape, qdtype)]
    return pl.pallas_call(
        functools.partial(_ffn_kernel, attn_out=attn_out is not None, final=final_g is not None,
                          query=query is not None),
        grid=(bsz // nb, s // t),
        in_specs=in_specs,
        out_specs=out_specs,
        out_shape=out_shape,
        compiler_params=_params("arbitrary", "arbitrary"),
        name="ffn",
    )(*args)


def _shift_rows(x, tail, j, row8):
    t = x.shape[1]
    r = pltpu.roll(x, j, axis=1)
    head = jnp.where(row8 < j, pltpu.roll(tail, j, axis=1), r[:, :SUBLANES])
    if t == SUBLANES:
        return head
    return jnp.concatenate([head, r[:, SUBLANES:]], axis=1)


def _rglru_kernel(x_ref, sh_ref, sc_ref, gt_ref, g_ref, conv0_ref, h0_ref,
                  win_ref, cw_ref, cb_ref, wg_ref, brg_ref, big_ref, lam_ref, wout_ref,
                  xo_ref, convo_ref, ho_ref,
                  tail_sc, h_sc, a_sc, u_sc, hs_sc):
    @pl.when(pl.program_id(1) == 0)
    def _():
        tail_sc[...] = conv0_ref[...]
        h_sc[...] = h0_ref[...]

    x = x_ref[...]
    nb, t, _ = x.shape
    m = nb * t
    h = _modulate(x, g_ref[...], sh_ref[...], sc_ref[...])
    xy = jnp.dot(h.reshape(m, D_MODEL).astype(BF16), win_ref[...], preferred_element_type=F32)
    xb = xy[:, :D_RNN].reshape(nb, t, D_RNN)
    yb = jax.nn.gelu(xy[:, D_RNN:])

    tail = tail_sc[...]
    row8 = lax.broadcasted_iota(jnp.int32, (1, SUBLANES, 1), 1)
    cw = cw_ref[...]
    xc = cb_ref[...] + cw[0:1] * _shift_rows(xb, tail, 3, row8)
    xc = xc + cw[1:2] * _shift_rows(xb, tail, 2, row8)
    xc = xc + cw[2:3] * _shift_rows(xb, tail, 1, row8)
    xc = xc + cw[3:4] * xb
    new_tail = xb[:, t - SUBLANES:, :]
    tail_sc[...] = new_tail
    convo_ref[...] = new_tail

    xc2 = xc.reshape(m, D_RNN)
    xcb = xc2.astype(BF16)
    pre = [jnp.dot(xcb[:, n * LRU_BLOCK:(n + 1) * LRU_BLOCK], wg_ref[n], preferred_element_type=F32)
           for n in range(N_LRU_BLOCKS)]
    rpre = jnp.concatenate([p[:, :LRU_BLOCK] for p in pre], axis=1)
    ipre = jnp.concatenate([p[:, LRU_BLOCK:] for p in pre], axis=1)
    r = jax.nn.sigmoid(rpre + brg_ref[...])
    i = jax.nn.sigmoid(ipre + big_ref[...])
    log_a = (-LRU_C * r) * jax.nn.softplus(-lam_ref[...])
    a_sc[...] = jnp.exp(log_a).reshape(nb, t, D_RNN)
    th = jnp.tanh(log_a)
    one_minus_a2 = (-2.0 * th) / (1.0 - th)
    u_sc[...] = (jnp.sqrt(one_minus_a2) * (i * xc2)).reshape(nb, t, D_RNN)

    def group(gi, hprev):
        s0 = pl.multiple_of(gi * SUBLANES, SUBLANES)
        ca = a_sc[:, pl.ds(s0, SUBLANES), :]
        cb = u_sc[:, pl.ds(s0, SUBLANES), :]
        for s in (1, 2, 4):
            keep = row8 >= s
            cb = jnp.where(keep, ca * pltpu.roll(cb, s, axis=1) + cb, cb)
            ca = jnp.where(keep, ca * pltpu.roll(ca, s, axis=1), ca)
        hs = ca * hprev + cb
        hs_sc[:, pl.ds(s0, SUBLANES), :] = hs
        return hs[:, SUBLANES - 1:SUBLANES, :]

    hlast = lax.fori_loop(0, t // SUBLANES, group, h_sc[...])
    h_sc[...] = hlast
    ho_ref[...] = hlast

    mixed = (hs_sc[...].reshape(m, D_RNN) * yb).astype(BF16)
    out = jnp.dot(mixed, wout_ref[...], preferred_element_type=F32).reshape(nb, t, D_MODEL)
    xo_ref[...] = x + gt_ref[...] * out


def _rglru_call(x, mods, row0, norm_g3, g_row, conv0, h0, w_in, conv_w, conv_b, w_gate, b_rg, b_ig,
                lam, w_out, l, nb, t):
    bsz, s, _ = x.shape
    vec = lambda: pl.BlockSpec((None, 1, D_RNN), lambda b, i: (l, 0, 0))
    return pl.pallas_call(
        _rglru_kernel,
        grid=(bsz // nb, s // t),
        in_specs=[
            _tok_spec(nb, t),
            _mod_spec(row0, nb), _mod_spec(row0 + 1, nb), _mod_spec(row0 + 2, nb),
            pl.BlockSpec((None, 1, D_MODEL), lambda b, i: (g_row, 0, 0)),
            pl.BlockSpec((None, nb, SUBLANES, D_RNN), lambda b, i: (l, b, 0, 0)),
            pl.BlockSpec((None, nb, 1, D_RNN), lambda b, i: (l, b, 0, 0)),
            _const_spec((None, D_MODEL, 2 * D_RNN), lambda b, i: (l, 0, 0)),
            pl.BlockSpec((None, CONV_WIDTH, D_RNN), lambda b, i: (l, 0, 0)),
            vec(),
            _const_spec((None, N_LRU_BLOCKS, LRU_BLOCK, 2 * LRU_BLOCK), lambda b, i: (l, 0, 0, 0)),
            vec(), vec(), vec(),
            _const_spec((None, D_RNN, D_MODEL), lambda b, i: (l, 0, 0)),
        ],
        out_specs=[
            _tok_spec(nb, t),
            pl.BlockSpec((nb, SUBLANES, D_RNN), lambda b, i: (b, 0, 0)),
            pl.BlockSpec((nb, 1, D_RNN), lambda b, i: (b, 0, 0)),
        ],
        out_shape=[
            jax.ShapeDtypeStruct(x.shape, F32),
            jax.ShapeDtypeStruct((bsz, SUBLANES, D_RNN), F32),
            jax.ShapeDtypeStruct((bsz, 1, D_RNN), F32),
        ],
        scratch_shapes=[
            pltpu.VMEM((nb, SUBLANES, D_RNN), F32),
            pltpu.VMEM((nb, 1, D_RNN), F32),
            pltpu.VMEM((nb, t, D_RNN), F32),
            pltpu.VMEM((nb, t, D_RNN), F32),
            pltpu.VMEM((nb, t, D_RNN), F32),
        ],
        compiler_params=_params("arbitrary", "arbitrary"),
        name="rglru",
    )(x, mods, mods, mods, norm_g3, conv0, h0, w_in, conv_w, conv_b, w_gate, b_rg, b_ig, lam, w_out)


def _rope(x, cos_ref, sin_ref):
    reps = D_MODEL // LANES
    cos = jnp.concatenate([cos_ref[...]] * reps, axis=-1)[None]
    sin = jnp.concatenate([sin_ref[...]] * reps, axis=-1)[None]
    lane = lax.broadcasted_iota(jnp.int32, (1, 1, D_MODEL), 2)
    low_half = (lane % HEAD_DIM) < (HEAD_DIM // 2)
    swapped = jnp.where(low_half,
                        pltpu.roll(x, D_MODEL - HEAD_DIM // 2, axis=2),
                        pltpu.roll(x, HEAD_DIM // 2, axis=2))
    return x * cos + swapped * sin


def _kv_kernel(x_ref, sh_ref, sc_ref, g_ref, w_ref, cos_ref, sin_ref, k_ref, v_ref, kb_ref, vb_ref):
    x = x_ref[...]
    nb, t, _ = x.shape
    h = _modulate(x, g_ref[...], sh_ref[...], sc_ref[...])
    y = jnp.dot(h.reshape(nb * t, D_MODEL).astype(BF16), w_ref[...], preferred_element_type=F32)
    k = _rope(y[:, :D_MODEL].reshape(nb, t, D_MODEL), cos_ref, sin_ref)
    v = y[:, D_MODEL:].reshape(nb, t, D_MODEL)
    k_ref[...] = k
    v_ref[...] = v
    kb_ref[...] = k.astype(BF16)
    vb_ref[...] = v.astype(BF16)


def _kv_call(x, kvmods, kv_norm_g, w_kv, cos, sin, nb, t):
    bsz, s, _ = x.shape
    rope_spec = pl.BlockSpec((t, LANES), lambda b, i: (i, 0))
    shp = lambda dt: jax.ShapeDtypeStruct(x.shape, dt)
    return pl.pallas_call(
        _kv_kernel,
        grid=(bsz // nb, s // t),
        in_specs=[_tok_spec(nb, t), _mod_spec(0, nb), _mod_spec(1, nb),
                  pl.BlockSpec((1, D_MODEL), lambda b, i: (0, 0)),
                  _const_spec((D_MODEL, 2 * D_MODEL), lambda b, i: (0, 0)),
                  rope_spec, rope_spec],
        out_specs=[_tok_spec(nb, t)] * 4,
        out_shape=[shp(F32), shp(F32), shp(BF16), shp(BF16)],
        compiler_params=_params("arbitrary", "arbitrary"),
        name="kv_proj",
    )(x, kvmods, kvmods, kv_norm_g, w_kv, cos, sin)


def _diff_lambda(lamp_ref, lam_init):
    lp = lamp_ref[...]
    e1 = jnp.exp(jnp.sum(lp[0:1] * lp[1:2], axis=-1, keepdims=True))
    e2 = jnp.exp(jnp.sum(lp[2:3] * lp[3:4], axis=-1, keepdims=True))
    return e1 - e2 + lam_init


def _subln(d, sg, lam_init):
    ms = jnp.mean(d * d, axis=-1, keepdims=True)
    return ((d * lax.rsqrt(ms + NORM_EPS)) * sg) * (1.0 - lam_init)


def _attn_prompt_kernel(q_ref, k_ref, v_ref, lamp_ref, sg_ref, o_ref, m_sc, acc_sc, *, lam_init, tq, hg, unroll2):
    qi = pl.program_id(2)
    lane = lax.broadcasted_iota(jnp.int32, (1, V_DIM), 1)
    qs = []
    for hd in range(hg):
        q = q_ref[:, hd * V_DIM:(hd + 1) * V_DIM]
        zero = jnp.zeros_like(q)
        qs += [jnp.where(lane < HEAD_DIM, q, zero), jnp.where(lane >= HEAD_DIM, q, zero)]
    m_sc[...] = jnp.full(m_sc.shape, -jnp.inf, F32)
    acc_sc[...] = jnp.zeros(acc_sc.shape, F32)
    ones = jnp.ones((tq, V_DIM), BF16)

    def step(off, masked):
        if masked:
            r = lax.broadcasted_iota(jnp.int32, (tq, tq), 0)
            c = lax.broadcasted_iota(jnp.int32, (tq, tq), 1)
            ok = c <= r
        scores = [lax.dot_general(qs[c], k_ref[pl.ds(off, tq), (c // 2) * V_DIM:(c // 2 + 1) * V_DIM],
                                  (((1,), (1,)), ((), ())), preferred_element_type=F32)
                  for c in range(2 * hg)]
        for c, s in enumerate(scores):
            rows = slice(c * tq, (c + 1) * tq)
            hd = c // 2
            vb = jnp.concatenate([v_ref[pl.ds(off, tq), hd * V_DIM:(hd + 1) * V_DIM], ones], axis=1)
            if masked:
                s = jnp.where(ok, s, -jnp.inf)
            m_prev = m_sc[rows]
            m_new = jnp.maximum(m_prev, jnp.max(s, axis=-1, keepdims=True))
            alpha = jnp.exp2(m_prev - m_new)
            p = jnp.exp2(s - jnp.concatenate([m_new] * (tq // LANES), axis=1))
            pv = jnp.dot(p.astype(BF16), vb, preferred_element_type=F32)
            acc_sc[rows] = jnp.concatenate([alpha, alpha], axis=1) * acc_sc[rows] + pv
            m_sc[rows] = m_new

    if unroll2:
        def body(j, carry):
            step(pl.multiple_of(2 * j * tq, tq), False)
            step(pl.multiple_of((2 * j + 1) * tq, tq), False)
            return carry

        lax.fori_loop(0, qi // 2, body, 0)

        @pl.when(qi % 2 == 1)
        def _():
            step(pl.multiple_of((qi - 1) * tq, tq), False)
    else:
        def body(j, carry):
            step(pl.multiple_of(j * tq, tq), False)
            return carry

        lax.fori_loop(0, qi, body, 0)

    step(pl.multiple_of(qi * tq, tq), True)

    lam = _diff_lambda(lamp_ref, lam_init)
    for hd in range(hg):
        acc = acc_sc[2 * hd * tq:(2 * hd + 2) * tq]
        o = acc[:, :V_DIM] / acc[:, V_DIM:]
        d = o[:tq] - lam * o[tq:]
        o_ref[:, hd * V_DIM:(hd + 1) * V_DIM] = _subln(d, sg_ref[...], lam_init).astype(o_ref.dtype)


def _attn_prompt_call(q, k, v, lam_p, subln_g, j, lam_init, tq, hg):
    bsz, s, _ = q.shape
    return pl.pallas_call(
        functools.partial(_attn_prompt_kernel, lam_init=lam_init, tq=tq, hg=hg, unroll2=ATTN_UNROLL_KEY_BLOCKS),
        grid=(bsz, N_HEADS // hg, s // tq),
        in_specs=[
            pl.BlockSpec((None, tq, hg * V_DIM), lambda b, h, i: (b, i, h)),
            pl.BlockSpec((None, s, hg * V_DIM), lambda b, h, i: (b, 0, h)),
            pl.BlockSpec((None, s, hg * V_DIM), lambda b, h, i: (b, 0, h)),
            pl.BlockSpec((None, 4, HEAD_DIM), lambda b, h, i: (j, 0, 0)),
            pl.BlockSpec((None, 1, V_DIM), lambda b, h, i: (j, 0, 0)),
        ],
        out_specs=pl.BlockSpec((None, tq, hg * V_DIM), lambda b, h, i: (b, i, h)),
        out_shape=jax.ShapeDtypeStruct(q.shape, BF16),
        scratch_shapes=[
            pltpu.VMEM((hg * 2 * tq, LANES), F32),
            pltpu.VMEM((hg * 2 * tq, 2 * V_DIM), F32),
        ],
        compiler_params=_params("arbitrary", "arbitrary", "arbitrary"),
        name="attn_prompt",
    )(q, k, v, lam_p, subln_g)


def _attn_decode_kernel(pt_ref, q_ref, *refs, pps, nstep, lam_init):
    k_refs = refs[:pps]
    v_refs = refs[pps:2 * pps]
    kn_ref, vn_ref, lamp_ref, sg_ref, o_ref, m_sc, l_sc, acc_sc, wt_sc = refs[2 * pps:]
    step_id = pl.program_id(1)
    nrow = 2 * N_HEADS * SUBLANES

    @pl.when(step_id == 0)
    def _():
        m_sc[...] = jnp.full(m_sc.shape, -jnp.inf, F32)
        l_sc[...] = jnp.zeros(l_sc.shape, F32)
        acc_sc[...] = jnp.zeros(acc_sc.shape, F32)
        q = q_ref[...]
        row = lax.broadcasted_iota(jnp.int32, (nrow, D_MODEL), 0)
        col = lax.broadcasted_iota(jnp.int32, (nrow, D_MODEL), 1)
        wt_sc[...] = jnp.where((row // SUBLANES) == (col // HEAD_DIM),
                               jnp.concatenate([q] * (nrow // SUBLANES), axis=0), 0.0).astype(BF16)

    hrows = 2 * SUBLANES

    def update(s, v_of):
        m_prev = m_sc[...]
        m_new = jnp.maximum(m_prev, jnp.max(s, axis=-1, keepdims=True))
        alpha = jnp.exp2(m_prev - m_new)
        n_pg = s.shape[1] // PAGE_SIZE
        p = jnp.exp2(s - jnp.concatenate([m_new] * n_pg, axis=1))
        l_sc[...] = alpha * l_sc[...] + jnp.sum(p, axis=-1, keepdims=True)
        p = p.astype(BF16)
        heads = []
        for hd in range(N_HEADS):
            ph = p[hd * hrows:(hd + 1) * hrows]
            pv = jnp.dot(ph[:, :PAGE_SIZE], v_of(0, hd), preferred_element_type=F32)
            for i in range(1, n_pg):
                pv = pv + jnp.dot(ph[:, i * PAGE_SIZE:(i + 1) * PAGE_SIZE], v_of(i, hd),
                                  preferred_element_type=F32)
            heads.append(pv)
        acc_sc[...] = alpha * acc_sc[...] + jnp.concatenate(heads, axis=0)
        m_sc[...] = m_new

    @pl.when(step_id < nstep)
    def _():
        wt = wt_sc[...]
        ss = [jnp.dot(wt, k_refs[i][...].astype(BF16), preferred_element_type=F32) for i in range(pps)]
        update(jnp.concatenate(ss, axis=1),
               lambda i, hd: v_refs[i][pl.ds(hd, PAGE_SIZE, stride=N_HEADS), :].astype(BF16))

    @pl.when(step_id == nstep)
    def _():
        s = lax.dot_general(wt_sc[...], kn_ref[...], (((1,), (1,)), ((), ())), preferred_element_type=F32)
        r = lax.broadcasted_iota(jnp.int32, (nrow, PAGE_SIZE), 0)
        c = lax.broadcasted_iota(jnp.int32, (nrow, PAGE_SIZE), 1)
        update(jnp.where(c <= (r % SUBLANES), s, -jnp.inf),
               lambda i, hd: vn_ref[:, hd * V_DIM:(hd + 1) * V_DIM])
        o = acc_sc[...] / l_sc[...]
        lam = _diff_lambda(lamp_ref, lam_init)
        heads = []
        for hd in range(N_HEADS):
            r0 = hd * hrows
            d = o[r0:r0 + SUBLANES] - lam * o[r0 + SUBLANES:r0 + hrows]
            heads.append(_subln(d, sg_ref[...], lam_init))
        o_ref[...] = jnp.concatenate(heads, axis=1)


def _attn_decode_call(q, cache_kt, cache_v, page_table, kn_pad, vn_pad, lam_p, subln_g, j, lam_init, pps):
    bsz = q.shape[0]
    n_pages = page_table.shape[1]
    nstep = n_pages // pps

    def page_spec(i, rows, cols):
        return pl.BlockSpec(
            (None, rows, cols),
            lambda b, s, pt: (pt[b, jnp.minimum(s * pps + i, n_pages - 1)], 0, 0))

    tok = pl.BlockSpec((None, SUBLANES, D_MODEL), lambda b, s, pt: (b, 0, 0))
    new = pl.BlockSpec((None, PAGE_SIZE, D_MODEL), lambda b, s, pt: (b, 0, 0))
    nrow = 2 * N_HEADS * SUBLANES
    grid_spec = pltpu.PrefetchScalarGridSpec(
        num_scalar_prefetch=1,
        grid=(bsz, nstep + 1),
        in_specs=[tok] + [page_spec(i, D_MODEL, PAGE_SIZE) for i in range(pps)]
        + [page_spec(i, PAGE_SIZE * N_HEADS, V_DIM) for i in range(pps)] + [
            new, new,
            pl.BlockSpec((None, 4, HEAD_DIM), lambda b, s, pt: (j, 0, 0)),
            pl.BlockSpec((None, 1, V_DIM), lambda b, s, pt: (j, 0, 0)),
        ],
        out_specs=tok,
        scratch_shapes=[
            pltpu.VMEM((nrow, LANES), F32),
            pltpu.VMEM((nrow, LANES), F32),
            pltpu.VMEM((nrow, V_DIM), F32),
            pltpu.VMEM((nrow, D_MODEL), BF16),
        ],
    )
    return pl.pallas_call(
        functools.partial(_attn_decode_kernel, pps=pps, nstep=nstep, lam_init=lam_init),
        grid_spec=grid_spec,
        out_shape=jax.ShapeDtypeStruct(q.shape, F32),
        compiler_params=_params("arbitrary", "arbitrary"),
        name="attn_decode",
    )(page_table, q, *([cache_kt] * pps), *([cache_v] * pps), kn_pad, vn_pad, lam_p, subln_g)


PROMPT_TILE = 512
ATTN_TILE = 512
ATTN_HEADS_PER_STEP = 2
ATTN_UNROLL_KEY_BLOCKS = True
PAGES_PER_STEP = 8
C_ROWS_PROMPT = 0
C_ROWS_SAMPLE = 8


def _rope_tables(pos):
    half = HEAD_DIM // 2
    inv = 1.0 / (ROPE_THETA ** (jnp.arange(half, dtype=F32) * (2.0 / HEAD_DIM)))
    ang = pos.astype(F32)[:, None] * inv[None, :]
    cos = jnp.tile(jnp.cos(ang), (1, LANES // half))
    sign = jnp.where((jnp.arange(LANES) % HEAD_DIM) < half, -1.0, 1.0).astype(F32)
    sin = jnp.tile(jnp.sin(ang), (1, LANES // half)) * sign[None, :]
    return cos, sin


def _group_rows(rows, lo, n):
    nl, _, kd = rows.shape
    k = kd // D_MODEL
    g = rows[:, lo:lo + n].reshape(nl, n, k, D_MODEL)
    return jnp.transpose(g, (0, 2, 1, 3)).reshape(nl * k, n, 1, D_MODEL)


def _run_trunk(x, mods, kvmods, conv0, h0, past_len, w, nb, t, attn_fn, q_dtype):
    bsz, s, _ = x.shape
    pos = past_len + jnp.arange(s, dtype=jnp.int32)
    cos, sin = _rope_tables(pos)
    convs, hs = [], []
    k_new = v_new = kb = vb = None
    for l in range(DEPTH):
        if l == N_A_LAYERS:
            k_new, v_new, kb, vb = _kv_call(x, kvmods, w["kv_norm_g"], w["w_kv"], cos, sin, nb, t)
        r0 = l * N_ADA
        if l < N_A_LAYERS:
            x = _ffn_call(x, mods, r0, w["norm_g"], l * 3, w["ffn_w_gu"], w["ffn_w_d"], l, 0, nb, t)
            x, cs, hl = _rglru_call(x, mods, r0 + 3, w["norm_g"], l * 3 + 1, conv0, h0,
                                    w["a_w_in"], w["a_conv_w"], w["a_conv_b"], w["a_w_gate"],
                                    w["a_b_rg"], w["a_b_ig"], w["a_lambda"], w["a_w_out"], l, nb, t)
            convs.append(cs[:, SUBLANES - (CONV_WIDTH - 1):])
            hs.append(hl[:, 0])
            attn_out = None
        else:
            j = l - N_A_LAYERS
            lam_init = 0.8 - 0.6 * math.exp(-0.3 * l)
            x, q = _ffn_call(x, mods, r0, w["norm_g"], l * 3, w["ffn_w_gu"], w["ffn_w_d"], l, 0, nb, t,
                             query=(r0 + 3, l * 3 + 1, w["b_w_q"], j, cos, sin, q_dtype))
            attn_out = (attn_fn(q, j, lam_init, kb, vb), r0 + 5, w["b_w_o"], j)
        x = _ffn_call(x, mods, r0 + 6, w["norm_g"], l * 3 + 2, w["ffn_w_gu"], w["ffn_w_d"], l, 1, nb, t,
                      final_g=w["final_g"] if l == DEPTH - 1 else None, attn_out=attn_out)
    k_new = k_new.reshape(bsz, s, N_HEADS, 2, HEAD_DIM)
    v_new = v_new.reshape(bsz, s, N_HEADS, V_DIM)
    return x, jnp.stack(convs, 0), jnp.stack(hs, 0), k_new, v_new


def kernel(x_prompt, x_sample, c_prompt, c_sample, state_conv, state_h, cache_k, cache_v, page_table, ada_w, ada_b, norm_g, ffn_w_gu, ffn_w_d, a_w_in, a_conv_w, a_conv_b, a_w_rg, a_b_rg, a_w_ig, a_b_ig, a_lambda, a_w_out, kv_ada_w, kv_ada_b, kv_norm_g, w_kv, b_w_q, b_lambda, b_subln_g, b_w_o, final_g):
    bp, sp, _ = x_prompt.shape
    bs, ss, _ = x_sample.shape
    assert ss == SUBLANES and sp % PROMPT_TILE == 0 and PROMPT_TILE == ATTN_TILE

    w = {
        "norm_g": norm_g.reshape(DEPTH * 3, 1, D_MODEL),
        "ffn_w_gu": ffn_w_gu.astype(BF16),
        "ffn_w_d": ffn_w_d.astype(BF16),
        "a_w_in": a_w_in.astype(BF16),
        "a_conv_w": a_conv_w,
        "a_conv_b": a_conv_b.reshape(N_A_LAYERS, 1, D_RNN),
        "a_w_gate": jnp.concatenate([a_w_rg, a_w_ig], axis=-1).astype(BF16),
        "a_b_rg": a_b_rg.reshape(N_A_LAYERS, 1, D_RNN),
        "a_b_ig": a_b_ig.reshape(N_A_LAYERS, 1, D_RNN),
        "a_lambda": a_lambda.reshape(N_A_LAYERS, 1, D_RNN),
        "a_w_out": a_w_out.astype(BF16),
        "kv_norm_g": kv_norm_g.reshape(1, D_MODEL),
        "w_kv": w_kv.astype(BF16),
        "b_w_q": b_w_q.astype(BF16),
        "b_w_o": b_w_o.astype(BF16),
        "final_g": final_g.reshape(1, D_MODEL),
    }
    subln_g = b_subln_g.reshape(-1, 1, V_DIM)

    pad = jnp.zeros((C_ROWS_SAMPLE - bp, D_MODEL), F32)
    c_all = jnp.concatenate([c_prompt, pad, c_sample], axis=0)
    rows = _ada_call(c_all, ada_w, ada_b.reshape(DEPTH, 1, -1), tn=N_ADA * D_MODEL // 4)
    kvrows = _ada_call(c_all, kv_ada_w[None], kv_ada_b.reshape(1, 1, -1), tn=2 * D_MODEL)

    def prompt_attn(q, j, lam_init, kb, vb):
        return _attn_prompt_call(q, kb, vb, b_lambda, subln_g, j, lam_init, ATTN_TILE, ATTN_HEADS_PER_STEP)

    y_p, conv_p, h_p, k_p, v_p = _run_trunk(
        x_prompt, _group_rows(rows, C_ROWS_PROMPT, bp), _group_rows(kvrows, C_ROWS_PROMPT, bp),
        jnp.zeros((N_A_LAYERS, bp, SUBLANES, D_RNN), F32), jnp.zeros((N_A_LAYERS, bp, 1, D_RNN), F32),
        0, w, 1, PROMPT_TILE, prompt_attn, BF16)

    n_phys = cache_k.shape[0]
    ck = jnp.transpose(cache_k, (0, 2, 3, 4, 1)).reshape(n_phys, D_MODEL, PAGE_SIZE)
    cv = cache_v.reshape(n_phys, PAGE_SIZE * N_HEADS, V_DIM)
    past_len = page_table.shape[1] * PAGE_SIZE

    def sample_attn(q, j, lam_init, kb, vb):
        padn = ((0, 0), (0, PAGE_SIZE - ss), (0, 0))
        return _attn_decode_call(q, ck, cv, page_table, jnp.pad(kb, padn), jnp.pad(vb, padn),
                                 b_lambda, subln_g, j, lam_init, PAGES_PER_STEP)

    conv0 = jnp.pad(state_conv, ((0, 0), (0, 0), (SUBLANES - (CONV_WIDTH - 1), 0), (0, 0)))
    y_s, conv_s, h_s, k_s, v_s = _run_trunk(
        x_sample, _group_rows(rows, C_ROWS_SAMPLE, bs), _group_rows(kvrows, C_ROWS_SAMPLE, bs),
        conv0, state_h[:, :, None, :], past_len, w, bs, ss, sample_attn, F32)

    return (y_p, y_s, conv_p, h_p, k_p, v_p, conv_s, h_s, k_s, v_s)
```

```python
import functools
import math

import jax
import jax.numpy as jnp
from jax import lax
from jax.experimental import pallas as pl
from jax.experimental.pallas import tpu as pltpu

D_MODEL = 1024
DEPTH = 4
N_A_LAYERS = DEPTH // 2
D_RNN = D_MODEL
N_LRU_BLOCKS = 8
LRU_BLOCK = D_RNN // N_LRU_BLOCKS
CONV_WIDTH = 4
LRU_C = 8.0
N_HEADS = 8
HEAD_DIM = D_MODEL // (2 * N_HEADS)
V_DIM = 2 * HEAD_DIM
D_FF = 2816
ROPE_THETA = 10000.0
NORM_EPS = 1e-6
N_ADA = 9
PAGE_SIZE = 128

SUBLANES = 8
LANES = 128
VMEM_LIMIT_BYTES = 56 * 1024 * 1024

LOG2_E = math.log2(math.e)

F32 = jnp.float32
BF16 = jnp.bfloat16


def _params(*sem):
    return pltpu.CompilerParams(dimension_semantics=sem, vmem_limit_bytes=VMEM_LIMIT_BYTES)


def _const_spec(shape, index_map):
    return pl.BlockSpec(shape, index_map, pipeline_mode=pl.Buffered(1))


def _modulate(x, g, shift, scale):
    ms = jnp.mean(x * x, axis=-1, keepdims=True)
    y = x * lax.rsqrt(ms + NORM_EPS)
    return (y * g) * (1.0 + scale) + shift


def _ada_kernel(c_ref, w_ref, b_ref, o_ref):
    c_act = jax.nn.silu(c_ref[...]).astype(BF16)
    o_ref[...] = jnp.dot(c_act, w_ref[...].astype(BF16), preferred_element_type=F32) + b_ref[...]


def _ada_call(c_all, w, b, tn):
    nl, _, n = w.shape
    r = c_all.shape[0]
    return pl.pallas_call(
        _ada_kernel,
        grid=(nl, n // tn),
        in_specs=[
            pl.BlockSpec((r, D_MODEL), lambda l, j: (0, 0)),
            pl.BlockSpec((None, D_MODEL, tn), lambda l, j: (l, 0, j)),
            pl.BlockSpec((None, 1, tn), lambda l, j: (l, 0, j)),
        ],
        out_specs=pl.BlockSpec((None, r, tn), lambda l, j: (l, 0, j)),
        out_shape=jax.ShapeDtypeStruct((nl, r, n), F32),
        compiler_params=_params("arbitrary", "arbitrary"),
        name="ada_rows",
    )(c_all, w, b)


def _ffn_kernel(*refs, attn_out, final, query):
    it = iter(refs)
    x_ref = next(it)
    if attn_out:
        ao_ref, ogt_ref, wo_ref = next(it), next(it), next(it)
    sh_ref, sc_ref, gt_ref, g_ref, wgu_ref, wd_ref = (next(it) for _ in range(6))
    if final:
        fg_ref = next(it)
    if query:
        qsh_ref, qsc_ref, qg_ref, wq_ref, cos_ref, sin_ref = (next(it) for _ in range(6))
    o_ref = next(it)
    x = x_ref[...]
    nb, t, _ = x.shape
    m = nb * t
    if attn_out:
        ao = jnp.dot(ao_ref[...].reshape(m, D_MODEL).astype(BF16), wo_ref[...], preferred_element_type=F32)
        x = x + ogt_ref[...] * ao.reshape(nb, t, D_MODEL)
    h = _modulate(x, g_ref[...], sh_ref[...], sc_ref[...])
    hb = h.reshape(m, D_MODEL).astype(BF16)
    gu = jnp.dot(hb, wgu_ref[...], preferred_element_type=F32)
    a = (jax.nn.silu(gu[:, :D_FF]) * gu[:, D_FF:]).astype(BF16)
    y = jnp.dot(a, wd_ref[...], preferred_element_type=F32).reshape(nb, t, D_MODEL)
    out = x + (0.5 * gt_ref[...]) * y
    if final:
        ms = jnp.mean(out * out, axis=-1, keepdims=True)
        out = (out * lax.rsqrt(ms + NORM_EPS)) * fg_ref[...]
    o_ref[...] = out
    if query:
        q_ref = next(it)
        hq = _modulate(out, qg_ref[...], qsh_ref[...], qsc_ref[...])
        yq = jnp.dot(hq.reshape(m, D_MODEL).astype(BF16), wq_ref[...], preferred_element_type=F32)
        q = _rope(yq.reshape(nb, t, D_MODEL), cos_ref, sin_ref) * (HEAD_DIM ** -0.5 * LOG2_E)
        q_ref[...] = q.astype(q_ref.dtype)


def _mod_spec(row, nb):
    return pl.BlockSpec((None, nb, 1, D_MODEL), lambda b, t: (row, b, 0, 0))


def _tok_spec(nb, t, width=D_MODEL):
    return pl.BlockSpec((nb, t, width), lambda b, i: (b, i, 0))


def _ffn_call(x, mods, row0, norm_g3, g_row, wgu, wd, l, j, nb, t, final_g=None, attn_out=None, query=None):
    bsz, s, _ = x.shape
    in_specs = [_tok_spec(nb, t)]
    args = [x]
    if attn_out is not None:
        o, gate_row, w_o, oj = attn_out
        in_specs += [_tok_spec(nb, t), _mod_spec(gate_row, nb),
                     _const_spec((None, D_MODEL, D_MODEL), lambda b, i: (oj, 0, 0))]
        args += [o, mods, w_o]
    in_specs += [
        _mod_spec(row0, nb), _mod_spec(row0 + 1, nb), _mod_spec(row0 + 2, nb),
        pl.BlockSpec((None, 1, D_MODEL), lambda b, i: (g_row, 0, 0)),
        _const_spec((None, None, D_MODEL, 2 * D_FF), lambda b, i: (l, j, 0, 0)),
        _const_spec((None, None, D_FF, D_MODEL), lambda b, i: (l, j, 0, 0)),
    ]
    args += [mods, mods, mods, norm_g3, wgu, wd]
    if final_g is not None:
        in_specs.append(pl.BlockSpec((1, D_MODEL), lambda b, i: (0, 0)))
        args.append(final_g)
    out_specs = _tok_spec(nb, t)
    out_shape = jax.ShapeDtypeStruct(x.shape, F32)
    if query is not None:
        qrow0, qg_row, w_q, qj, cos, sin, qdtype = query
        rope_spec = pl.BlockSpec((t, LANES), lambda b, i: (i, 0))
        in_specs += [_mod_spec(qrow0, nb), _mod_spec(qrow0 + 1, nb),
                     pl.BlockSpec((None, 1, D_MODEL), lambda b, i: (qg_row, 0, 0)),
                     _const_spec((None, D_MODEL, D_MODEL), lambda b, i: (qj, 0, 0)),
                     rope_spec, rope_spec]
        args += [mods, mods, norm_g3, w_q, cos, sin]
        out_specs = [out_specs, _tok_spec(nb, t)]
        out_shape = [out_shape, jax.ShapeDtypeStruct(x.shape, qdtype)]
    return pl.pallas_call(
        functools.partial(_ffn_kernel, attn_out=attn_out is not None, final=final_g is not None,
                          query=query is not None),
        grid=(bsz // nb, s // t),
        in_specs=in_specs,
        out_specs=out_specs,
        out_shape=out_shape,
        compiler_params=_params("arbitrary", "arbitrary"),
        name="ffn",
    )(*args)


def _shift_rows(x, tail, j, row8):
    t = x.shape[1]
    r = pltpu.roll(x, j, axis=1)
    head = jnp.where(row8 < j, pltpu.roll(tail, j, axis=1), r[:, :SUBLANES])
    if t == SUBLANES:
        return head
    return jnp.concatenate([head, r[:, SUBLANES:]], axis=1)


def _rglru_kernel(x_ref, sh_ref, sc_ref, gt_ref, g_ref, conv0_ref, h0_ref,
                  win_ref, cw_ref, cb_ref, wg_ref, brg_ref, big_ref, lam_ref, wout_ref,
                  xo_ref, convo_ref, ho_ref,
                  tail_sc, h_sc):
    @pl.when(pl.program_id(1) == 0)
    def _():
        tail_sc[...] = conv0_ref[...]
        h_sc[...] = h0_ref[...]

    x = x_ref[...]
    nb, t, _ = x.shape
    m = nb * t
    h = _modulate(x, g_ref[...], sh_ref[...], sc_ref[...])
    xy = jnp.dot(h.reshape(m, D_MODEL).astype(BF16), win_ref[...], preferred_element_type=F32)
    xb = xy[:, :D_RNN].reshape(nb, t, D_RNN)
    yb = jax.nn.gelu(xy[:, D_RNN:])

    tail = tail_sc[...]
    row8 = lax.broadcasted_iota(jnp.int32, (1, SUBLANES, 1), 1)
    cw = cw_ref[...]
    xc = cb_ref[...] + cw[0:1] * _shift_rows(xb, tail, 3, row8)
    xc = xc + cw[1:2] * _shift_rows(xb, tail, 2, row8)
    xc = xc + cw[2:3] * _shift_rows(xb, tail, 1, row8)
    xc = xc + cw[3:4] * xb
    new_tail = xb[:, t - SUBLANES:, :]
    tail_sc[...] = new_tail
    convo_ref[...] = new_tail

    xc2 = xc.reshape(m, D_RNN)
    xcb = xc2.astype(BF16)
    pre = [jnp.dot(xcb[:, n * LRU_BLOCK:(n + 1) * LRU_BLOCK], wg_ref[n], preferred_element_type=F32)
           for n in range(N_LRU_BLOCKS)]
    rpre = jnp.concatenate([p[:, :LRU_BLOCK] for p in pre], axis=1)
    ipre = jnp.concatenate([p[:, LRU_BLOCK:] for p in pre], axis=1)
    r = jax.nn.sigmoid(rpre + brg_ref[...])
    i = jax.nn.sigmoid(ipre + big_ref[...])
    log_a = (-LRU_C * r) * jax.nn.softplus(-lam_ref[...])
    a3 = jnp.exp(log_a).reshape(nb, t, D_RNN)
    th = jnp.tanh(log_a)
    one_minus_a2 = (-2.0 * th) / (1.0 - th)
    u3 = (jnp.sqrt(one_minus_a2) * (i * xc2)).reshape(nb, t, D_RNN)

    hprev = h_sc[...]
    hs_groups = []
    for gi in range(t // SUBLANES):
        rows = slice(gi * SUBLANES, (gi + 1) * SUBLANES)
        ca = a3[:, rows, :]
        cb = u3[:, rows, :]
        for s in (1, 2, 4):
            keep = row8 >= s
            cb = jnp.where(keep, ca * pltpu.roll(cb, s, axis=1) + cb, cb)
            ca = jnp.where(keep, ca * pltpu.roll(ca, s, axis=1), ca)
        hs = ca * hprev + cb
        hs_groups.append(hs)
        hprev = hs[:, SUBLANES - 1:SUBLANES, :]
    h_sc[...] = hprev
    ho_ref[...] = hprev
    hs_all = hs_groups[0] if len(hs_groups) == 1 else jnp.concatenate(hs_groups, axis=1)

    mixed = (hs_all.reshape(m, D_RNN) * yb).astype(BF16)
    out = jnp.dot(mixed, wout_ref[...], preferred_element_type=F32).reshape(nb, t, D_MODEL)
    xo_ref[...] = x + gt_ref[...] * out


def _rglru_call(x, mods, row0, norm_g3, g_row, conv0, h0, w_in, conv_w, conv_b, w_gate, b_rg, b_ig,
                lam, w_out, l, nb, t):
    bsz, s, _ = x.shape
    vec = lambda: pl.BlockSpec((None, 1, D_RNN), lambda b, i: (l, 0, 0))
    return pl.pallas_call(
        _rglru_kernel,
        grid=(bsz // nb, s // t),
        in_specs=[
            _tok_spec(nb, t),
            _mod_spec(row0, nb), _mod_spec(row0 + 1, nb), _mod_spec(row0 + 2, nb),
            pl.BlockSpec((None, 1, D_MODEL), lambda b, i: (g_row, 0, 0)),
            pl.BlockSpec((None, nb, SUBLANES, D_RNN), lambda b, i: (l, b, 0, 0)),
            pl.BlockSpec((None, nb, 1, D_RNN), lambda b, i: (l, b, 0, 0)),
            _const_spec((None, D_MODEL, 2 * D_RNN), lambda b, i: (l, 0, 0)),
            pl.BlockSpec((None, CONV_WIDTH, D_RNN), lambda b, i: (l, 0, 0)),
            vec(),
            _const_spec((None, N_LRU_BLOCKS, LRU_BLOCK, 2 * LRU_BLOCK), lambda b, i: (l, 0, 0, 0)),
            vec(), vec(), vec(),
            _const_spec((None, D_RNN, D_MODEL), lambda b, i: (l, 0, 0)),
        ],
        out_specs=[
            _tok_spec(nb, t),
            pl.BlockSpec((nb, SUBLANES, D_RNN), lambda b, i: (b, 0, 0)),
            pl.BlockSpec((nb, 1, D_RNN), lambda b, i: (b, 0, 0)),
        ],
        out_shape=[
            jax.ShapeDtypeStruct(x.shape, F32),
            jax.ShapeDtypeStruct((bsz, SUBLANES, D_RNN), F32),
            jax.ShapeDtypeStruct((bsz, 1, D_RNN), F32),
        ],
        scratch_shapes=[
            pltpu.VMEM((nb, SUBLANES, D_RNN), F32),
            pltpu.VMEM((nb, 1, D_RNN), F32),
        ],
        compiler_params=_params("arbitrary", "arbitrary"),
        name="rglru",
    )(x, mods, mods, mods, norm_g3, conv0, h0, w_in, conv_w, conv_b, w_gate, b_rg, b_ig, lam, w_out)


def _rope(x, cos_ref, sin_ref):
    reps = D_MODEL // LANES
    cos = jnp.concatenate([cos_ref[...]] * reps, axis=-1)[None]
    sin = jnp.concatenate([sin_ref[...]] * reps, axis=-1)[None]
    lane = lax.broadcasted_iota(jnp.int32, (1, 1, D_MODEL), 2)
    low_half = (lane % HEAD_DIM) < (HEAD_DIM // 2)
    swapped = jnp.where(low_half,
                        pltpu.roll(x, D_MODEL - HEAD_DIM // 2, axis=2),
                        pltpu.roll(x, HEAD_DIM // 2, axis=2))
    return x * cos + swapped * sin


def _kv_kernel(x_ref, sh_ref, sc_ref, g_ref, w_ref, cos_ref, sin_ref, k_ref, v_ref, kb_ref, vb_ref):
    x = x_ref[...]
    nb, t, _ = x.shape
    h = _modulate(x, g_ref[...], sh_ref[...], sc_ref[...])
    y = jnp.dot(h.reshape(nb * t, D_MODEL).astype(BF16), w_ref[...], preferred_element_type=F32)
    k = _rope(y[:, :D_MODEL].reshape(nb, t, D_MODEL), cos_ref, sin_ref)
    v = y[:, D_MODEL:].reshape(nb, t, D_MODEL)
    k_ref[...] = k
    v_ref[...] = v
    kb_ref[...] = k.astype(BF16)
    vb_ref[...] = v.astype(BF16)


def _kv_call(x, kvmods, kv_norm_g, w_kv, cos, sin, nb, t):
    bsz, s, _ = x.shape
    rope_spec = pl.BlockSpec((t, LANES), lambda b, i: (i, 0))
    shp = lambda dt: jax.ShapeDtypeStruct(x.shape, dt)
    return pl.pallas_call(
        _kv_kernel,
        grid=(bsz // nb, s // t),
        in_specs=[_tok_spec(nb, t), _mod_spec(0, nb), _mod_spec(1, nb),
                  pl.BlockSpec((1, D_MODEL), lambda b, i: (0, 0)),
                  _const_spec((D_MODEL, 2 * D_MODEL), lambda b, i: (0, 0)),
                  rope_spec, rope_spec],
        out_specs=[_tok_spec(nb, t)] * 4,
        out_shape=[shp(F32), shp(F32), shp(BF16), shp(BF16)],
        compiler_params=_params("arbitrary", "arbitrary"),
        name="kv_proj",
    )(x, kvmods, kvmods, kv_norm_g, w_kv, cos, sin)


def _diff_lambda(lamp_ref, lam_init):
    lp = lamp_ref[...]
    e1 = jnp.exp(jnp.sum(lp[0:1] * lp[1:2], axis=-1, keepdims=True))
    e2 = jnp.exp(jnp.sum(lp[2:3] * lp[3:4], axis=-1, keepdims=True))
    return e1 - e2 + lam_init


def _subln(d, sg, lam_init):
    ms = jnp.mean(d * d, axis=-1, keepdims=True)
    return ((d * lax.rsqrt(ms + NORM_EPS)) * sg) * (1.0 - lam_init)


def _attn_prompt_kernel(q_ref, k_ref, v_ref, lamp_ref, sg_ref, o_ref, m_sc, acc_sc, *, lam_init, tq, hg):
    qi = pl.program_id(2)
    lane = lax.broadcasted_iota(jnp.int32, (1, V_DIM), 1)
    qs = []
    for hd in range(hg):
        q = q_ref[:, hd * V_DIM:(hd + 1) * V_DIM]
        zero = jnp.zeros_like(q)
        qs += [jnp.where(lane < HEAD_DIM, q, zero), jnp.where(lane >= HEAD_DIM, q, zero)]
    ones = jnp.ones((tq, V_DIM), BF16)

    def step(off, diagonal):
        if diagonal:
            r = lax.broadcasted_iota(jnp.int32, (tq, tq), 0)
            c = lax.broadcasted_iota(jnp.int32, (tq, tq), 1)
            ok = c <= r
        scores = [lax.dot_general(qs[c], k_ref[pl.ds(off, tq), (c // 2) * V_DIM:(c // 2 + 1) * V_DIM],
                                  (((1,), (1,)), ((), ())), preferred_element_type=F32)
                  for c in range(2 * hg)]
        for c, s in enumerate(scores):
            rows = slice(c * tq, (c + 1) * tq)
            hd = c // 2
            vb = jnp.concatenate([v_ref[pl.ds(off, tq), hd * V_DIM:(hd + 1) * V_DIM], ones], axis=1)
            if diagonal:
                s = jnp.where(ok, s, -jnp.inf)
                m_new = jnp.broadcast_to(jnp.max(s, axis=-1, keepdims=True), (tq, LANES))
            else:
                m_prev = m_sc[rows]
                m_new = jnp.maximum(m_prev, jnp.max(s, axis=-1, keepdims=True))
                alpha = jnp.exp2(m_prev - m_new)
            p = jnp.exp2(s - jnp.concatenate([m_new] * (tq // LANES), axis=1))
            pv = jnp.dot(p.astype(BF16), vb, preferred_element_type=F32)
            if diagonal:
                acc_sc[rows] = pv
            else:
                acc_sc[rows] = jnp.concatenate([alpha, alpha], axis=1) * acc_sc[rows] + pv
            m_sc[rows] = m_new

    step(pl.multiple_of(qi * tq, tq), True)

    def body(j, carry):
        step(pl.multiple_of(2 * j * tq, tq), False)
        step(pl.multiple_of((2 * j + 1) * tq, tq), False)
        return carry

    lax.fori_loop(0, qi // 2, body, 0)

    @pl.when(qi % 2 == 1)
    def _():
        step(pl.multiple_of((qi - 1) * tq, tq), False)

    lam = _diff_lambda(lamp_ref, lam_init)
    for hd in range(hg):
        acc = acc_sc[2 * hd * tq:(2 * hd + 2) * tq]
        o = acc[:, :V_DIM] / acc[:, V_DIM:]
        d = o[:tq] - lam * o[tq:]
        o_ref[:, hd * V_DIM:(hd + 1) * V_DIM] = _subln(d, sg_ref[...], lam_init).astype(o_ref.dtype)


def _attn_prompt_call(q, k, v, lam_p, subln_g, j, lam_init, tq, hg):
    bsz, s, _ = q.shape
    return pl.pallas_call(
        functools.partial(_attn_prompt_kernel, lam_init=lam_init, tq=tq, hg=hg),
        grid=(bsz, N_HEADS // hg, s // tq),
        in_specs=[
            pl.BlockSpec((None, tq, hg * V_DIM), lambda b, h, i: (b, i, h)),
            pl.BlockSpec((None, s, hg * V_DIM), lambda b, h, i: (b, 0, h)),
            pl.BlockSpec((None, s, hg * V_DIM), lambda b, h, i: (b, 0, h)),
            pl.BlockSpec((None, 4, HEAD_DIM), lambda b, h, i: (j, 0, 0)),
            pl.BlockSpec((None, 1, V_DIM), lambda b, h, i: (j, 0, 0)),
        ],
        out_specs=pl.BlockSpec((None, tq, hg * V_DIM), lambda b, h, i: (b, i, h)),
        out_shape=jax.ShapeDtypeStruct(q.shape, BF16),
        scratch_shapes=[
            pltpu.VMEM((hg * 2 * tq, LANES), F32),
            pltpu.VMEM((hg * 2 * tq, 2 * V_DIM), F32),
        ],
        compiler_params=_params("arbitrary", "arbitrary", "arbitrary"),
        name="attn_prompt",
    )(q, k, v, lam_p, subln_g)


def _attn_decode_kernel(pt_ref, q_ref, *refs, pps, nstep, lam_init):
    k_refs = refs[:pps]
    v_refs = refs[pps:2 * pps]
    kn_ref, vn_ref, lamp_ref, sg_ref, o_ref, m_sc, l_sc, acc_sc, wt_sc = refs[2 * pps:]
    step_id = pl.program_id(1)
    nrow = 2 * N_HEADS * SUBLANES

    @pl.when(step_id == 0)
    def _():
        m_sc[...] = jnp.full(m_sc.shape, -jnp.inf, F32)
        l_sc[...] = jnp.zeros(l_sc.shape, F32)
        acc_sc[...] = jnp.zeros(acc_sc.shape, F32)
        q = q_ref[...]
        row = lax.broadcasted_iota(jnp.int32, (nrow, D_MODEL), 0)
        col = lax.broadcasted_iota(jnp.int32, (nrow, D_MODEL), 1)
        wt_sc[...] = jnp.where((row // SUBLANES) == (col // HEAD_DIM),
                               jnp.concatenate([q] * (nrow // SUBLANES), axis=0), 0.0).astype(BF16)

    hrows = 2 * SUBLANES

    def update(s, v_of):
        m_prev = m_sc[...]
        m_new = jnp.maximum(m_prev, jnp.max(s, axis=-1, keepdims=True))
        alpha = jnp.exp2(m_prev - m_new)
        n_pg = s.shape[1] // PAGE_SIZE
        p = jnp.exp2(s - jnp.concatenate([m_new] * n_pg, axis=1))
        l_sc[...] = alpha * l_sc[...] + jnp.sum(p, axis=-1, keepdims=True)
        p = p.astype(BF16)
        heads = []
        for hd in range(N_HEADS):
            ph = p[hd * hrows:(hd + 1) * hrows]
            pv = jnp.dot(ph[:, :PAGE_SIZE], v_of(0, hd), preferred_element_type=F32)
            for i in range(1, n_pg):
                pv = pv + jnp.dot(ph[:, i * PAGE_SIZE:(i + 1) * PAGE_SIZE], v_of(i, hd),
                                  preferred_element_type=F32)
            heads.append(pv)
        acc_sc[...] = alpha * acc_sc[...] + jnp.concatenate(heads, axis=0)
        m_sc[...] = m_new

    @pl.when(step_id < nstep)
    def _():
        wt = wt_sc[...]
        ss = [jnp.dot(wt, k_refs[i][...].astype(BF16), preferred_element_type=F32) for i in range(pps)]
        vh = [pltpu.einshape("(th)d->htd", v_refs[i][...].astype(BF16), h=N_HEADS) for i in range(pps)]
        update(jnp.concatenate(ss, axis=1), lambda i, hd: vh[i][hd])

    @pl.when(step_id == nstep)
    def _():
        s = lax.dot_general(wt_sc[...], kn_ref[...], (((1,), (1,)), ((), ())), preferred_element_type=F32)
        r = lax.broadcasted_iota(jnp.int32, (nrow, PAGE_SIZE), 0)
        c = lax.broadcasted_iota(jnp.int32, (nrow, PAGE_SIZE), 1)
        update(jnp.where(c <= (r % SUBLANES), s, -jnp.inf),
               lambda i, hd: vn_ref[:, hd * V_DIM:(hd + 1) * V_DIM])
        o = acc_sc[...] / l_sc[...]
        lam = _diff_lambda(lamp_ref, lam_init)
        heads = []
        for hd in range(N_HEADS):
            r0 = hd * hrows
            d = o[r0:r0 + SUBLANES] - lam * o[r0 + SUBLANES:r0 + hrows]
            heads.append(_subln(d, sg_ref[...], lam_init))
        o_ref[...] = jnp.concatenate(heads, axis=1)


def _attn_decode_call(q, cache_kt, cache_v, page_table, kn_pad, vn_pad, lam_p, subln_g, j, lam_init, pps):
    bsz = q.shape[0]
    n_pages = page_table.shape[1]
    nstep = n_pages // pps

    def page_spec(i, rows, cols):
        return pl.BlockSpec(
            (None, rows, cols),
            lambda b, s, pt: (pt[b, jnp.minimum(s * pps + i, n_pages - 1)], 0, 0),
        )

    tok = pl.BlockSpec((None, SUBLANES, D_MODEL), lambda b, s, pt: (b, 0, 0))
    new = pl.BlockSpec((None, PAGE_SIZE, D_MODEL), lambda b, s, pt: (b, 0, 0))
    nrow = 2 * N_HEADS * SUBLANES
    grid_spec = pltpu.PrefetchScalarGridSpec(
        num_scalar_prefetch=1,
        grid=(bsz, nstep + 1),
        in_specs=[tok] + [page_spec(i, D_MODEL, PAGE_SIZE) for i in range(pps)]
        + [page_spec(i, PAGE_SIZE * N_HEADS, V_DIM) for i in range(pps)] + [
            new, new,
            pl.BlockSpec((None, 4, HEAD_DIM), lambda b, s, pt: (j, 0, 0)),
            pl.BlockSpec((None, 1, V_DIM), lambda b, s, pt: (j, 0, 0)),
        ],
        out_specs=tok,
        scratch_shapes=[
            pltpu.VMEM((nrow, LANES), F32),
            pltpu.VMEM((nrow, LANES), F32),
            pltpu.VMEM((nrow, V_DIM), F32),
            pltpu.VMEM((nrow, D_MODEL), BF16),
        ],
    )
    return pl.pallas_call(
        functools.partial(_attn_decode_kernel, pps=pps, nstep=nstep, lam_init=lam_init),
        grid_spec=grid_spec,
        out_shape=jax.ShapeDtypeStruct(q.shape, F32),
        compiler_params=_params("arbitrary", "arbitrary"),
        name="attn_decode",
    )(page_table, q, *([cache_kt] * pps), *([cache_v] * pps), kn_pad, vn_pad, lam_p, subln_g)


PROMPT_TILE = 512
ATTN_TILE = 512
ATTN_HEADS_PER_STEP = 2
PAGES_PER_STEP = 8
C_ROWS_PROMPT = 0
C_ROWS_SAMPLE = 8


def _rope_tables(pos):
    half = HEAD_DIM // 2
    inv = 1.0 / (ROPE_THETA ** (jnp.arange(half, dtype=F32) * (2.0 / HEAD_DIM)))
    ang = pos.astype(F32)[:, None] * inv[None, :]
    cos = jnp.tile(jnp.cos(ang), (1, LANES // half))
    sign = jnp.where((jnp.arange(LANES) % HEAD_DIM) < half, -1.0, 1.0).astype(F32)
    sin = jnp.tile(jnp.sin(ang), (1, LANES // half)) * sign[None, :]
    return cos, sin


def _group_rows(rows, lo, n):
    nl, _, kd = rows.shape
    k = kd // D_MODEL
    g = rows[:, lo:lo + n].reshape(nl, n, k, D_MODEL)
    return jnp.transpose(g, (0, 2, 1, 3)).reshape(nl * k, n, 1, D_MODEL)


def _run_trunk(x, mods, kvmods, conv0, h0, past_len, w, nb, t, attn_fn, q_dtype):
    bsz, s, _ = x.shape
    pos = past_len + jnp.arange(s, dtype=jnp.int32)
    cos, sin = _rope_tables(pos)
    convs, hs = [], []
    k_new = v_new = kb = vb = None
    for l in range(DEPTH):
        if l == N_A_LAYERS:
            k_new, v_new, kb, vb = _kv_call(x, kvmods, w["kv_norm_g"], w["w_kv"], cos, sin, nb, t)
        r0 = l * N_ADA
        if l < N_A_LAYERS:
            x = _ffn_call(x, mods, r0, w["norm_g"], l * 3, w["ffn_w_gu"], w["ffn_w_d"], l, 0, nb, t)
            x, cs, hl = _rglru_call(x, mods, r0 + 3, w["norm_g"], l * 3 + 1, conv0, h0,
                                    w["a_w_in"], w["a_conv_w"], w["a_conv_b"], w["a_w_gate"],
                                    w["a_b_rg"], w["a_b_ig"], w["a_lambda"], w["a_w_out"], l, nb, t)
            convs.append(cs[:, SUBLANES - (CONV_WIDTH - 1):])
            hs.append(hl[:, 0])
            attn_out = None
        else:
            j = l - N_A_LAYERS
            lam_init = 0.8 - 0.6 * math.exp(-0.3 * l)
            x, q = _ffn_call(x, mods, r0, w["norm_g"], l * 3, w["ffn_w_gu"], w["ffn_w_d"], l, 0, nb, t,
                             query=(r0 + 3, l * 3 + 1, w["b_w_q"], j, cos, sin, q_dtype))
            attn_out = (attn_fn(q, j, lam_init, kb, vb), r0 + 5, w["b_w_o"], j)
        x = _ffn_call(x, mods, r0 + 6, w["norm_g"], l * 3 + 2, w["ffn_w_gu"], w["ffn_w_d"], l, 1, nb, t,
                      final_g=w["final_g"] if l == DEPTH - 1 else None, attn_out=attn_out)
    k_new = k_new.reshape(bsz, s, N_HEADS, 2, HEAD_DIM)
    v_new = v_new.reshape(bsz, s, N_HEADS, V_DIM)
    return x, jnp.stack(convs, 0), jnp.stack(hs, 0), k_new, v_new


def kernel(x_prompt, x_sample, c_prompt, c_sample, state_conv, state_h, cache_k, cache_v, page_table, ada_w, ada_b, norm_g, ffn_w_gu, ffn_w_d, a_w_in, a_conv_w, a_conv_b, a_w_rg, a_b_rg, a_w_ig, a_b_ig, a_lambda, a_w_out, kv_ada_w, kv_ada_b, kv_norm_g, w_kv, b_w_q, b_lambda, b_subln_g, b_w_o, final_g):
    bp, sp, _ = x_prompt.shape
    bs, ss, _ = x_sample.shape
    assert ss == SUBLANES and sp % PROMPT_TILE == 0 and PROMPT_TILE == ATTN_TILE

    w = {
        "norm_g": norm_g.reshape(DEPTH * 3, 1, D_MODEL),
        "ffn_w_gu": ffn_w_gu.astype(BF16),
        "ffn_w_d": ffn_w_d.astype(BF16),
        "a_w_in": a_w_in.astype(BF16),
        "a_conv_w": a_conv_w,
        "a_conv_b": a_conv_b.reshape(N_A_LAYERS, 1, D_RNN),
        "a_w_gate": jnp.concatenate([a_w_rg, a_w_ig], axis=-1).astype(BF16),
        "a_b_rg": a_b_rg.reshape(N_A_LAYERS, 1, D_RNN),
        "a_b_ig": a_b_ig.reshape(N_A_LAYERS, 1, D_RNN),
        "a_lambda": a_lambda.reshape(N_A_LAYERS, 1, D_RNN),
        "a_w_out": a_w_out.astype(BF16),
        "kv_norm_g": kv_norm_g.reshape(1, D_MODEL),
        "w_kv": w_kv.astype(BF16),
        "b_w_q": b_w_q.astype(BF16),
        "b_w_o": b_w_o.astype(BF16),
        "final_g": final_g.reshape(1, D_MODEL),
    }
    subln_g = b_subln_g.reshape(-1, 1, V_DIM)

    pad = jnp.zeros((C_ROWS_SAMPLE - bp, D_MODEL), F32)
    c_all = jnp.concatenate([c_prompt, pad, c_sample], axis=0)
    rows = _ada_call(c_all, ada_w, ada_b.reshape(DEPTH, 1, -1), tn=N_ADA * D_MODEL // 4)
    kvrows = _ada_call(c_all, kv_ada_w[None], kv_ada_b.reshape(1, 1, -1), tn=2 * D_MODEL)

    def prompt_attn(q, j, lam_init, kb, vb):
        return _attn_prompt_call(q, kb, vb, b_lambda, subln_g, j, lam_init, ATTN_TILE, ATTN_HEADS_PER_STEP)

    y_p, conv_p, h_p, k_p, v_p = _run_trunk(
        x_prompt, _group_rows(rows, C_ROWS_PROMPT, bp), _group_rows(kvrows, C_ROWS_PROMPT, bp),
        jnp.zeros((N_A_LAYERS, bp, SUBLANES, D_RNN), F32), jnp.zeros((N_A_LAYERS, bp, 1, D_RNN), F32),
        0, w, 1, PROMPT_TILE, prompt_attn, BF16)

    n_phys = cache_k.shape[0]
    ck = jnp.transpose(cache_k, (0, 2, 3, 4, 1)).reshape(n_phys, D_MODEL, PAGE_SIZE)
    cv = cache_v.reshape(n_phys, PAGE_SIZE * N_HEADS, V_DIM)
    past_len = page_table.shape[1] * PAGE_SIZE

    def sample_attn(q, j, lam_init, kb, vb):
        padn = ((0, 0), (0, PAGE_SIZE - ss), (0, 0))
        return _attn_decode_call(q, ck, cv, page_table, jnp.pad(kb, padn), jnp.pad(vb, padn),
                                 b_lambda, subln_g, j, lam_init, PAGES_PER_STEP)

    conv0 = jnp.pad(state_conv, ((0, 0), (0, 0), (SUBLANES - (CONV_WIDTH - 1), 0), (0, 0)))
    y_s, conv_s, h_s, k_s, v_s = _run_trunk(
        x_sample, _group_rows(rows, C_ROWS_SAMPLE, bs), _group_rows(kvrows, C_ROWS_SAMPLE, bs),
        conv0, state_h[:, :, None, :], past_len, w, bs, ss, sample_attn, F32)

    return (y_p, y_s, conv_p, h_p, k_p, v_p, conv_s, h_s, k_s, v_s)
```

```python
import functools
import math

import jax
import jax.numpy as jnp
from jax import lax
from jax.experimental import pallas as pl
from jax.experimental.pallas import tpu as pltpu

D_MODEL = 1024
DEPTH = 4
N_A_LAYERS = DEPTH // 2
D_RNN = D_MODEL
N_LRU_BLOCKS = 8
LRU_BLOCK = D_RNN // N_LRU_BLOCKS
CONV_WIDTH = 4
LRU_C = 8.0
N_HEADS = 8
HEAD_DIM = D_MODEL // (2 * N_HEADS)
V_DIM = 2 * HEAD_DIM
D_FF = 2816
ROPE_THETA = 10000.0
NORM_EPS = 1e-6
N_ADA = 9
PAGE_SIZE = 128

SUBLANES = 8
LANES = 128
VMEM_LIMIT_BYTES = 56 * 1024 * 1024

LOG2_E = math.log2(math.e)

F32 = jnp.float32
BF16 = jnp.bfloat16


def _params(*sem):
    return pltpu.CompilerParams(dimension_semantics=sem, vmem_limit_bytes=VMEM_LIMIT_BYTES)


def _const_spec(shape, index_map):
    return pl.BlockSpec(shape, index_map, pipeline_mode=pl.Buffered(1))


def _modulate(x, g, shift, scale):
    ms = jnp.mean(x * x, axis=-1, keepdims=True)
    y = x * lax.rsqrt(ms + NORM_EPS)
    return (y * g) * (1.0 + scale) + shift


def _ada_kernel(c_ref, w_ref, b_ref, o_ref):
    c_act = jax.nn.silu(c_ref[...]).astype(BF16)
    o_ref[...] = jnp.dot(c_act, w_ref[...].astype(BF16), preferred_element_type=F32) + b_ref[...]


def _ada_call(c_all, w, b, tn):
    nl, _, n = w.shape
    r = c_all.shape[0]
    return pl.pallas_call(
        _ada_kernel,
        grid=(nl, n // tn),
        in_specs=[
            pl.BlockSpec((r, D_MODEL), lambda l, j: (0, 0)),
            pl.BlockSpec((None, D_MODEL, tn), lambda l, j: (l, 0, j)),
            pl.BlockSpec((None, 1, tn), lambda l, j: (l, 0, j)),
        ],
        out_specs=pl.BlockSpec((None, r, tn), lambda l, j: (l, 0, j)),
        out_shape=jax.ShapeDtypeStruct((nl, r, n), F32),
        compiler_params=_params("arbitrary", "arbitrary"),
        name="ada_rows",
    )(c_all, w, b)


def _ffn_kernel(*refs, attn_out, final, query):
    it = iter(refs)
    x_ref = next(it)
    if attn_out:
        ao_ref, ogt_ref, wo_ref = next(it), next(it), next(it)
    sh_ref, sc_ref, gt_ref, g_ref, wgu_ref, wd_ref = (next(it) for _ in range(6))
    if final:
        fg_ref = next(it)
    if query:
        qsh_ref, qsc_ref, qg_ref, wq_ref, cos_ref, sin_ref = (next(it) for _ in range(6))
    o_ref = next(it)
    x = x_ref[...]
    nb, t, _ = x.shape
    m = nb * t
    if attn_out:
        ao = jnp.dot(ao_ref[...].reshape(m, D_MODEL).astype(BF16), wo_ref[...], preferred_element_type=F32)
        x = x + ogt_ref[...] * ao.reshape(nb, t, D_MODEL)
    h = _modulate(x, g_ref[...], sh_ref[...], sc_ref[...])
    hb = h.reshape(m, D_MODEL).astype(BF16)
    gu = jnp.dot(hb, wgu_ref[...], preferred_element_type=F32)
    a = (jax.nn.silu(gu[:, :D_FF]) * gu[:, D_FF:]).astype(BF16)
    y = jnp.dot(a, wd_ref[...], preferred_element_type=F32).reshape(nb, t, D_MODEL)
    out = x + (0.5 * gt_ref[...]) * y
    if final:
        ms = jnp.mean(out * out, axis=-1, keepdims=True)
        out = (out * lax.rsqrt(ms + NORM_EPS)) * fg_ref[...]
    o_ref[...] = out
    if query:
        q_ref = next(it)
        hq = _modulate(out, qg_ref[...], qsh_ref[...], qsc_ref[...])
        yq = jnp.dot(hq.reshape(m, D_MODEL).astype(BF16), wq_ref[...], preferred_element_type=F32)
        q = _rope(yq.reshape(nb, t, D_MODEL), cos_ref, sin_ref) * (HEAD_DIM ** -0.5 * LOG2_E)
        q_ref[...] = q.astype(q_ref.dtype)


def _mod_spec(row, nb):
    return pl.BlockSpec((None, nb, 1, D_MODEL), lambda b, t: (row, b, 0, 0))


def _tok_spec(nb, t, width=D_MODEL):
    return pl.BlockSpec((nb, t, width), lambda b, i: (b, i, 0))


def _ffn_call(x, mods, row0, norm_g3, g_row, wgu, wd, l, j, nb, t, final_g=None, attn_out=None, query=None):
    bsz, s, _ = x.shape
    in_specs = [_tok_spec(nb, t)]
    args = [x]
    if attn_out is not None:
        o, gate_row, w_o, oj = attn_out
        in_specs += [_tok_spec(nb, t), _mod_spec(gate_row, nb),
                     _const_spec((None, D_MODEL, D_MODEL), lambda b, i: (oj, 0, 0))]
        args += [o, mods, w_o]
    in_specs += [
        _mod_spec(row0, nb), _mod_spec(row0 + 1, nb), _mod_spec(row0 + 2, nb),
        pl.BlockSpec((None, 1, D_MODEL), lambda b, i: (g_row, 0, 0)),
        _const_spec((None, None, D_MODEL, 2 * D_FF), lambda b, i: (l, j, 0, 0)),
        _const_spec((None, None, D_FF, D_MODEL), lambda b, i: (l, j, 0, 0)),
    ]
    args += [mods, mods, mods, norm_g3, wgu, wd]
    if final_g is not None:
        in_specs.append(pl.BlockSpec((1, D_MODEL), lambda b, i: (0, 0)))
        args.append(final_g)
    out_specs = _tok_spec(nb, t)
    out_shape = jax.ShapeDtypeStruct(x.shape, F32)
    if query is not None:
        qrow0, qg_row, w_q, qj, cos, sin, qdtype = query
        rope_spec = pl.BlockSpec((t, LANES), lambda b, i: (i, 0))
        in_specs += [_mod_spec(qrow0, nb), _mod_spec(qrow0 + 1, nb),
                     pl.BlockSpec((None, 1, D_MODEL), lambda b, i: (qg_row, 0, 0)),
                     _const_spec((None, D_MODEL, D_MODEL), lambda b, i: (qj, 0, 0)),
                     rope_spec, rope_spec]
        args += [mods, mods, norm_g3, w_q, cos, sin]
        out_specs = [out_specs, _tok_spec(nb, t)]
        out_shape = [out_shape, jax.ShapeDtypeStruct(x.shape, qdtype)]
    return pl.pallas_call(
        functools.partial(_ffn_kernel, attn_out=attn_out is not None, final=final_g is not None,
                          query=query is not None),
        grid=(bsz // nb, s // t),
        in_specs=in_specs,
        out_specs=out_specs,
        out_shape=out_shape,
        compiler_params=_params("arbitrary", "arbitrary"),
        name="ffn",
    )(*args)


def _shift_rows(x, tail, j, row8):
    t = x.shape[1]
    r = pltpu.roll(x, j, axis=1)
    head = jnp.where(row8 < j, pltpu.roll(tail, j, axis=1), r[:, :SUBLANES])
    if t == SUBLANES:
        return head
    return jnp.concatenate([head, r[:, SUBLANES:]], axis=1)


def _rglru_kernel(x_ref, sh_ref, sc_ref, gt_ref, g_ref, conv0_ref, h0_ref,
                  win_ref, cw_ref, cb_ref, wg_ref, brg_ref, big_ref, lam_ref, wout_ref,
                  xo_ref, convo_ref, ho_ref,
                  tail_sc, h_sc):
    @pl.when(pl.program_id(1) == 0)
    def _():
        tail_sc[...] = conv0_ref[...]
        h_sc[...] = h0_ref[...]

    x = x_ref[...]
    nb, t, _ = x.shape
    m = nb * t
    h = _modulate(x, g_ref[...], sh_ref[...], sc_ref[...])
    xy = jnp.dot(h.reshape(m, D_MODEL).astype(BF16), win_ref[...], preferred_element_type=F32)
    xb = xy[:, :D_RNN].reshape(nb, t, D_RNN)
    yb = jax.nn.gelu(xy[:, D_RNN:])

    tail = tail_sc[...]
    row8 = lax.broadcasted_iota(jnp.int32, (1, SUBLANES, 1), 1)
    cw = cw_ref[...]
    xc = cb_ref[...] + cw[0:1] * _shift_rows(xb, tail, 3, row8)
    xc = xc + cw[1:2] * _shift_rows(xb, tail, 2, row8)
    xc = xc + cw[2:3] * _shift_rows(xb, tail, 1, row8)
    xc = xc + cw[3:4] * xb
    new_tail = xb[:, t - SUBLANES:, :]
    tail_sc[...] = new_tail
    convo_ref[...] = new_tail

    xc2 = xc.reshape(m, D_RNN)
    xcb = xc2.astype(BF16)
    pre = [jnp.dot(xcb[:, n * LRU_BLOCK:(n + 1) * LRU_BLOCK], wg_ref[n], preferred_element_type=F32)
           for n in range(N_LRU_BLOCKS)]
    rpre = jnp.concatenate([p[:, :LRU_BLOCK] for p in pre], axis=1)
    ipre = jnp.concatenate([p[:, LRU_BLOCK:] for p in pre], axis=1)
    r = jax.nn.sigmoid(rpre + brg_ref[...])
    i = jax.nn.sigmoid(ipre + big_ref[...])
    log_a = (-LRU_C * r) * jax.nn.softplus(-lam_ref[...])
    a3 = jnp.exp(log_a).reshape(nb, t, D_RNN)
    th = jnp.tanh(log_a)
    one_minus_a2 = (-2.0 * th) / (1.0 - th)
    u3 = (jnp.sqrt(one_minus_a2) * (i * xc2)).reshape(nb, t, D_RNN)

    hprev = h_sc[...]
    hs_groups = []
    for gi in range(t // SUBLANES):
        rows = slice(gi * SUBLANES, (gi + 1) * SUBLANES)
        ca = a3[:, rows, :]
        cb = u3[:, rows, :]
        for s in (1, 2, 4):
            keep = row8 >= s
            cb = jnp.where(keep, ca * pltpu.roll(cb, s, axis=1) + cb, cb)
            ca = jnp.where(keep, ca * pltpu.roll(ca, s, axis=1), ca)
        hs = ca * hprev + cb
        hs_groups.append(hs)
        hprev = hs[:, SUBLANES - 1:SUBLANES, :]
    h_sc[...] = hprev
    ho_ref[...] = hprev
    hs_all = hs_groups[0] if len(hs_groups) == 1 else jnp.concatenate(hs_groups, axis=1)

    mixed = (hs_all.reshape(m, D_RNN) * yb).astype(BF16)
    out = jnp.dot(mixed, wout_ref[...], preferred_element_type=F32).reshape(nb, t, D_MODEL)
    xo_ref[...] = x + gt_ref[...] * out


def _rglru_call(x, mods, row0, norm_g3, g_row, conv0, h0, w_in, conv_w, conv_b, w_gate, b_rg, b_ig,
                lam, w_out, l, nb, t):
    bsz, s, _ = x.shape
    vec = lambda: pl.BlockSpec((None, 1, D_RNN), lambda b, i: (l, 0, 0))
    return pl.pallas_call(
        _rglru_kernel,
        grid=(bsz // nb, s // t),
        in_specs=[
            _tok_spec(nb, t),
            _mod_spec(row0, nb), _mod_spec(row0 + 1, nb), _mod_spec(row0 + 2, nb),
            pl.BlockSpec((None, 1, D_MODEL), lambda b, i: (g_row, 0, 0)),
            pl.BlockSpec((None, nb, SUBLANES, D_RNN), lambda b, i: (l, b, 0, 0)),
            pl.BlockSpec((None, nb, 1, D_RNN), lambda b, i: (l, b, 0, 0)),
            _const_spec((None, D_MODEL, 2 * D_RNN), lambda b, i: (l, 0, 0)),
            pl.BlockSpec((None, CONV_WIDTH, D_RNN), lambda b, i: (l, 0, 0)),
            vec(),
            _const_spec((None, N_LRU_BLOCKS, LRU_BLOCK, 2 * LRU_BLOCK), lambda b, i: (l, 0, 0, 0)),
            vec(), vec(), vec(),
            _const_spec((None, D_RNN, D_MODEL), lambda b, i: (l, 0, 0)),
        ],
        out_specs=[
            _tok_spec(nb, t),
            pl.BlockSpec((nb, SUBLANES, D_RNN), lambda b, i: (b, 0, 0)),
            pl.BlockSpec((nb, 1, D_RNN), lambda b, i: (b, 0, 0)),
        ],
        out_shape=[
            jax.ShapeDtypeStruct(x.shape, F32),
            jax.ShapeDtypeStruct((bsz, SUBLANES, D_RNN), F32),
            jax.ShapeDtypeStruct((bsz, 1, D_RNN), F32),
        ],
        scratch_shapes=[
            pltpu.VMEM((nb, SUBLANES, D_RNN), F32),
            pltpu.VMEM((nb, 1, D_RNN), F32),
        ],
        compiler_params=_params("arbitrary", "arbitrary"),
        name="rglru",
    )(x, mods, mods, mods, norm_g3, conv0, h0, w_in, conv_w, conv_b, w_gate, b_rg, b_ig, lam, w_out)


def _rope(x, cos_ref, sin_ref):
    reps = D_MODEL // LANES
    cos = jnp.concatenate([cos_ref[...]] * reps, axis=-1)[None]
    sin = jnp.concatenate([sin_ref[...]] * reps, axis=-1)[None]
    lane = lax.broadcasted_iota(jnp.int32, (1, 1, D_MODEL), 2)
    low_half = (lane % HEAD_DIM) < (HEAD_DIM // 2)
    swapped = jnp.where(low_half,
                        pltpu.roll(x, D_MODEL - HEAD_DIM // 2, axis=2),
                        pltpu.roll(x, HEAD_DIM // 2, axis=2))
    return x * cos + swapped * sin


def _kv_kernel(x_ref, sh_ref, sc_ref, g_ref, w_ref, cos_ref, sin_ref, k_ref, v_ref, kb_ref, vb_ref):
    x = x_ref[...]
    nb, t, _ = x.shape
    h = _modulate(x, g_ref[...], sh_ref[...], sc_ref[...])
    y = jnp.dot(h.reshape(nb * t, D_MODEL).astype(BF16), w_ref[...], preferred_element_type=F32)
    k = _rope(y[:, :D_MODEL].reshape(nb, t, D_MODEL), cos_ref, sin_ref)
    v = y[:, D_MODEL:].reshape(nb, t, D_MODEL)
    k_ref[...] = k
    v_ref[...] = v
    kb_ref[...] = k.astype(BF16)
    vb_ref[...] = v.astype(BF16)


def _kv_call(x, kvmods, kv_norm_g, w_kv, cos, sin, nb, t):
    bsz, s, _ = x.shape
    rope_spec = pl.BlockSpec((t, LANES), lambda b, i: (i, 0))
    shp = lambda dt: jax.ShapeDtypeStruct(x.shape, dt)
    return pl.pallas_call(
        _kv_kernel,
        grid=(bsz // nb, s // t),
        in_specs=[_tok_spec(nb, t), _mod_spec(0, nb), _mod_spec(1, nb),
                  pl.BlockSpec((1, D_MODEL), lambda b, i: (0, 0)),
                  _const_spec((D_MODEL, 2 * D_MODEL), lambda b, i: (0, 0)),
                  rope_spec, rope_spec],
        out_specs=[_tok_spec(nb, t)] * 4,
        out_shape=[shp(F32), shp(F32), shp(BF16), shp(BF16)],
        compiler_params=_params("arbitrary", "arbitrary"),
        name="kv_proj",
    )(x, kvmods, kvmods, kv_norm_g, w_kv, cos, sin)


def _diff_lambda(lamp_ref, lam_init):
    lp = lamp_ref[...]
    e1 = jnp.exp(jnp.sum(lp[0:1] * lp[1:2], axis=-1, keepdims=True))
    e2 = jnp.exp(jnp.sum(lp[2:3] * lp[3:4], axis=-1, keepdims=True))
    return e1 - e2 + lam_init


def _subln(d, sg, lam_init):
    ms = jnp.mean(d * d, axis=-1, keepdims=True)
    return ((d * lax.rsqrt(ms + NORM_EPS)) * sg) * (1.0 - lam_init)


def _attn_prompt_kernel(q_ref, k_ref, v_ref, lamp_ref, sg_ref, o_ref, m_sc, acc_sc, *, lam_init, tq, hg):
    qi = pl.program_id(2)
    lane = lax.broadcasted_iota(jnp.int32, (1, V_DIM), 1)
    qs = []
    for hd in range(hg):
        q = q_ref[:, hd * V_DIM:(hd + 1) * V_DIM]
        zero = jnp.zeros_like(q)
        qs += [jnp.where(lane < HEAD_DIM, q, zero), jnp.where(lane >= HEAD_DIM, q, zero)]
    ones = jnp.ones((tq, V_DIM), BF16)

    nt = (((1,), (1,)), ((), ()))

    def step(off):
        scores = [lax.dot_general(qs[c], k_ref[pl.ds(off, tq), (c // 2) * V_DIM:(c // 2 + 1) * V_DIM], nt,
                                  preferred_element_type=F32) for c in range(2 * hg)]
        for c, s in enumerate(scores):
            rows = slice(c * tq, (c + 1) * tq)
            hd = c // 2
            vb = jnp.concatenate([v_ref[pl.ds(off, tq), hd * V_DIM:(hd + 1) * V_DIM], ones], axis=1)
            m_prev = m_sc[rows]
            m_new = jnp.maximum(m_prev, jnp.max(s, axis=-1, keepdims=True))
            alpha = jnp.exp2(m_prev - m_new)
            p = jnp.exp2(s - jnp.concatenate([m_new] * (tq // LANES), axis=1))
            pv = jnp.dot(p.astype(BF16), vb, preferred_element_type=F32)
            acc_sc[rows] = jnp.concatenate([alpha, alpha], axis=1) * acc_sc[rows] + pv
            m_sc[rows] = m_new

    def diagonal_step(off):
        hq = tq // 2
        pieces = []
        for c in range(2 * hg):
            pieces += [(c, slice(0, hq), hq), (c, slice(hq, tq), tq)]
        scores = [lax.dot_general(qs[c][qr], k_ref[pl.ds(off, nk), (c // 2) * V_DIM:(c // 2 + 1) * V_DIM], nt,
                                  preferred_element_type=F32) for c, qr, nk in pieces]
        for (c, qr, nk), s in zip(pieces, scores):
            hd = c // 2
            r = lax.broadcasted_iota(jnp.int32, (hq, nk), 0) + qr.start
            col = lax.broadcasted_iota(jnp.int32, (hq, nk), 1)
            s = jnp.where(col <= r, s, -jnp.inf)
            m_new = jnp.broadcast_to(jnp.max(s, axis=-1, keepdims=True), (hq, LANES))
            p = jnp.exp2(s - jnp.concatenate([m_new] * (nk // LANES), axis=1))
            vb = jnp.concatenate([v_ref[pl.ds(off, nk), hd * V_DIM:(hd + 1) * V_DIM], ones[:nk]], axis=1)
            rows = slice(c * tq + qr.start, c * tq + qr.stop)
            acc_sc[rows] = jnp.dot(p.astype(BF16), vb, preferred_element_type=F32)
            m_sc[rows] = m_new

    diagonal_step(pl.multiple_of(qi * tq, tq))

    def body(j, carry):
        step(pl.multiple_of(2 * j * tq, tq))
        step(pl.multiple_of((2 * j + 1) * tq, tq))
        return carry

    lax.fori_loop(0, qi // 2, body, 0)

    @pl.when(qi % 2 == 1)
    def _():
        step(pl.multiple_of((qi - 1) * tq, tq))

    lam = _diff_lambda(lamp_ref, lam_init)
    for hd in range(hg):
        acc = acc_sc[2 * hd * tq:(2 * hd + 2) * tq]
        o = acc[:, :V_DIM] / acc[:, V_DIM:]
        d = o[:tq] - lam * o[tq:]
        o_ref[:, hd * V_DIM:(hd + 1) * V_DIM] = _subln(d, sg_ref[...], lam_init).astype(o_ref.dtype)


def _attn_prompt_call(q, k, v, lam_p, subln_g, j, lam_init, tq, hg):
    bsz, s, _ = q.shape
    return pl.pallas_call(
        functools.partial(_attn_prompt_kernel, lam_init=lam_init, tq=tq, hg=hg),
        grid=(bsz, N_HEADS // hg, s // tq),
        in_specs=[
            pl.BlockSpec((None, tq, hg * V_DIM), lambda b, h, i: (b, i, h)),
            pl.BlockSpec((None, s, hg * V_DIM), lambda b, h, i: (b, 0, h)),
            pl.BlockSpec((None, s, hg * V_DIM), lambda b, h, i: (b, 0, h)),
            pl.BlockSpec((None, 4, HEAD_DIM), lambda b, h, i: (j, 0, 0)),
            pl.BlockSpec((None, 1, V_DIM), lambda b, h, i: (j, 0, 0)),
        ],
        out_specs=pl.BlockSpec((None, tq, hg * V_DIM), lambda b, h, i: (b, i, h)),
        out_shape=jax.ShapeDtypeStruct(q.shape, BF16),
        scratch_shapes=[
            pltpu.VMEM((hg * 2 * tq, LANES), F32),
            pltpu.VMEM((hg * 2 * tq, 2 * V_DIM), F32),
        ],
        compiler_params=_params("arbitrary", "arbitrary", "arbitrary"),
        name="attn_prompt",
    )(q, k, v, lam_p, subln_g)


def _attn_decode_kernel(pt_ref, q_ref, *refs, pps, nstep, lam_init):
    k_refs = refs[:pps]
    v_refs = refs[pps:2 * pps]
    kn_ref, vn_ref, lamp_ref, sg_ref, o_ref, m_sc, l_sc, acc_sc, wt_sc = refs[2 * pps:]
    step_id = pl.program_id(1)
    nrow = 2 * N_HEADS * SUBLANES

    @pl.when(step_id == 0)
    def _():
        m_sc[...] = jnp.full(m_sc.shape, -jnp.inf, F32)
        l_sc[...] = jnp.zeros(l_sc.shape, F32)
        acc_sc[...] = jnp.zeros(acc_sc.shape, F32)
        q = q_ref[...]
        row = lax.broadcasted_iota(jnp.int32, (nrow, D_MODEL), 0)
        col = lax.broadcasted_iota(jnp.int32, (nrow, D_MODEL), 1)
        wt_sc[...] = jnp.where((row // SUBLANES) == (col // HEAD_DIM),
                               jnp.concatenate([q] * (nrow // SUBLANES), axis=0), 0.0).astype(BF16)

    hrows = 2 * SUBLANES

    def update(s, v_of):
        m_prev = m_sc[...]
        m_new = jnp.maximum(m_prev, jnp.max(s, axis=-1, keepdims=True))
        alpha = jnp.exp2(m_prev - m_new)
        n_pg = s.shape[1] // PAGE_SIZE
        p = jnp.exp2(s - jnp.concatenate([m_new] * n_pg, axis=1))
        l_sc[...] = alpha * l_sc[...] + jnp.sum(p, axis=-1, keepdims=True)
        p = p.astype(BF16)
        heads = []
        for hd in range(N_HEADS):
            ph = p[hd * hrows:(hd + 1) * hrows]
            pv = jnp.dot(ph[:, :PAGE_SIZE], v_of(0, hd), preferred_element_type=F32)
            for i in range(1, n_pg):
                pv = pv + jnp.dot(ph[:, i * PAGE_SIZE:(i + 1) * PAGE_SIZE], v_of(i, hd),
                                  preferred_element_type=F32)
            heads.append(pv)
        acc_sc[...] = alpha * acc_sc[...] + jnp.concatenate(heads, axis=0)
        m_sc[...] = m_new

    @pl.when(step_id < nstep)
    def _():
        wt = wt_sc[...]
        ss = [jnp.dot(wt, k_refs[i][...].astype(BF16), preferred_element_type=F32) for i in range(pps)]
        vh = [jnp.transpose(v_refs[i][...].astype(BF16).reshape(PAGE_SIZE, N_HEADS, V_DIM), (1, 0, 2))
              for i in range(pps)]
        update(jnp.concatenate(ss, axis=1), lambda i, hd: vh[i][hd])

    @pl.when(step_id == nstep)
    def _():
        s = lax.dot_general(wt_sc[...], kn_ref[...], (((1,), (1,)), ((), ())), preferred_element_type=F32)
        r = lax.broadcasted_iota(jnp.int32, (nrow, PAGE_SIZE), 0)
        c = lax.broadcasted_iota(jnp.int32, (nrow, PAGE_SIZE), 1)
        update(jnp.where(c <= (r % SUBLANES), s, -jnp.inf),
               lambda i, hd: vn_ref[:, hd * V_DIM:(hd + 1) * V_DIM])
        o = acc_sc[...] / l_sc[...]
        lam = _diff_lambda(lamp_ref, lam_init)
        heads = []
        for hd in range(N_HEADS):
            r0 = hd * hrows
            d = o[r0:r0 + SUBLANES] - lam * o[r0 + SUBLANES:r0 + hrows]
            heads.append(_subln(d, sg_ref[...], lam_init))
        o_ref[...] = jnp.concatenate(heads, axis=1)


def _attn_decode_call(q, cache_kt, cache_v, page_table, kn_pad, vn_pad, lam_p, subln_g, j, lam_init, pps):
    bsz = q.shape[0]
    n_pages = page_table.shape[1]
    nstep = n_pages // pps

    def page_spec(i, rows, cols):
        return pl.BlockSpec(
            (None, rows, cols),
            lambda b, s, pt: (pt[b, jnp.minimum(s * pps + i, n_pages - 1)], 0, 0),
        )

    tok = pl.BlockSpec((None, SUBLANES, D_MODEL), lambda b, s, pt: (b, 0, 0))
    new = pl.BlockSpec((None, PAGE_SIZE, D_MODEL), lambda b, s, pt: (b, 0, 0))
    nrow = 2 * N_HEADS * SUBLANES
    grid_spec = pltpu.PrefetchScalarGridSpec(
        num_scalar_prefetch=1,
        grid=(bsz, nstep + 1),
        in_specs=[tok] + [page_spec(i, D_MODEL, PAGE_SIZE) for i in range(pps)]
        + [page_spec(i, PAGE_SIZE * N_HEADS, V_DIM) for i in range(pps)] + [
            new, new,
            pl.BlockSpec((None, 4, HEAD_DIM), lambda b, s, pt: (j, 0, 0)),
            pl.BlockSpec((None, 1, V_DIM), lambda b, s, pt: (j, 0, 0)),
        ],
        out_specs=tok,
        scratch_shapes=[
            pltpu.VMEM((nrow, LANES), F32),
            pltpu.VMEM((nrow, LANES), F32),
            pltpu.VMEM((nrow, V_DIM), F32),
            pltpu.VMEM((nrow, D_MODEL), BF16),
        ],
    )
    return pl.pallas_call(
        functools.partial(_attn_decode_kernel, pps=pps, nstep=nstep, lam_init=lam_init),
        grid_spec=grid_spec,
        out_shape=jax.ShapeDtypeStruct(q.shape, F32),
        compiler_params=_params("arbitrary", "arbitrary"),
        name="attn_decode",
    )(page_table, q, *([cache_kt] * pps), *([cache_v] * pps), kn_pad, vn_pad, lam_p, subln_g)


PROMPT_TILE = 512
ATTN_TILE = 512
ATTN_HEADS_PER_STEP = 2
PAGES_PER_STEP = 8
C_ROWS_PROMPT = 0
C_ROWS_SAMPLE = 8


def _rope_tables(pos):
    half = HEAD_DIM // 2
    inv = 1.0 / (ROPE_THETA ** (jnp.arange(half, dtype=F32) * (2.0 / HEAD_DIM)))
    ang = pos.astype(F32)[:, None] * inv[None, :]
    cos = jnp.tile(jnp.cos(ang), (1, LANES // half))
    sign = jnp.where((jnp.arange(LANES) % HEAD_DIM) < half, -1.0, 1.0).astype(F32)
    sin = jnp.tile(jnp.sin(ang), (1, LANES // half)) * sign[None, :]
    return cos, sin


def _group_rows(rows, lo, n):
    nl, _, kd = rows.shape
    k = kd // D_MODEL
    g = rows[:, lo:lo + n].reshape(nl, n, k, D_MODEL)
    return jnp.transpose(g, (0, 2, 1, 3)).reshape(nl * k, n, 1, D_MODEL)


def _run_trunk(x, mods, kvmods, conv0, h0, past_len, w, nb, t, attn_fn, q_dtype):
    bsz, s, _ = x.shape
    pos = past_len + jnp.arange(s, dtype=jnp.int32)
    cos, sin = _rope_tables(pos)
    convs, hs = [], []
    k_new = v_new = kb = vb = None
    for l in range(DEPTH):
        if l == N_A_LAYERS:
            k_new, v_new, kb, vb = _kv_call(x, kvmods, w["kv_norm_g"], w["w_kv"], cos, sin, nb, t)
        r0 = l * N_ADA
        if l < N_A_LAYERS:
            x = _ffn_call(x, mods, r0, w["norm_g"], l * 3, w["ffn_w_gu"], w["ffn_w_d"], l, 0, nb, t)
            x, cs, hl = _rglru_call(x, mods, r0 + 3, w["norm_g"], l * 3 + 1, conv0, h0,
                                    w["a_w_in"], w["a_conv_w"], w["a_conv_b"], w["a_w_gate"],
                                    w["a_b_rg"], w["a_b_ig"], w["a_lambda"], w["a_w_out"], l, nb, t)
            convs.append(cs[:, SUBLANES - (CONV_WIDTH - 1):])
            hs.append(hl[:, 0])
            attn_out = None
        else:
            j = l - N_A_LAYERS
            lam_init = 0.8 - 0.6 * math.exp(-0.3 * l)
            x, q = _ffn_call(x, mods, r0, w["norm_g"], l * 3, w["ffn_w_gu"], w["ffn_w_d"], l, 0, nb, t,
                             query=(r0 + 3, l * 3 + 1, w["b_w_q"], j, cos, sin, q_dtype))
            attn_out = (attn_fn(q, j, lam_init, kb, vb), r0 + 5, w["b_w_o"], j)
        x = _ffn_call(x, mods, r0 + 6, w["norm_g"], l * 3 + 2, w["ffn_w_gu"], w["ffn_w_d"], l, 1, nb, t,
                      final_g=w["final_g"] if l == DEPTH - 1 else None, attn_out=attn_out)
    k_new = k_new.reshape(bsz, s, N_HEADS, 2, HEAD_DIM)
    v_new = v_new.reshape(bsz, s, N_HEADS, V_DIM)
    return x, jnp.stack(convs, 0), jnp.stack(hs, 0), k_new, v_new


def kernel(x_prompt, x_sample, c_prompt, c_sample, state_conv, state_h, cache_k, cache_v, page_table, ada_w, ada_b, norm_g, ffn_w_gu, ffn_w_d, a_w_in, a_conv_w, a_conv_b, a_w_rg, a_b_rg, a_w_ig, a_b_ig, a_lambda, a_w_out, kv_ada_w, kv_ada_b, kv_norm_g, w_kv, b_w_q, b_lambda, b_subln_g, b_w_o, final_g):
    bp, sp, _ = x_prompt.shape
    bs, ss, _ = x_sample.shape
    assert ss == SUBLANES and sp % PROMPT_TILE == 0 and PROMPT_TILE == ATTN_TILE

    w = {
        "norm_g": norm_g.reshape(DEPTH * 3, 1, D_MODEL),
        "ffn_w_gu": ffn_w_gu.astype(BF16),
        "ffn_w_d": ffn_w_d.astype(BF16),
        "a_w_in": a_w_in.astype(BF16),
        "a_conv_w": a_conv_w,
        "a_conv_b": a_conv_b.reshape(N_A_LAYERS, 1, D_RNN),
        "a_w_gate": jnp.concatenate([a_w_rg, a_w_ig], axis=-1).astype(BF16),
        "a_b_rg": a_b_rg.reshape(N_A_LAYERS, 1, D_RNN),
        "a_b_ig": a_b_ig.reshape(N_A_LAYERS, 1, D_RNN),
        "a_lambda": a_lambda.reshape(N_A_LAYERS, 1, D_RNN),
        "a_w_out": a_w_out.astype(BF16),
        "kv_norm_g": kv_norm_g.reshape(1, D_MODEL),
        "w_kv": w_kv.astype(BF16),
        "b_w_q": b_w_q.astype(BF16),
        "b_w_o": b_w_o.astype(BF16),
        "final_g": final_g.reshape(1, D_MODEL),
    }
    subln_g = b_subln_g.reshape(-1, 1, V_DIM)

    pad = jnp.zeros((C_ROWS_SAMPLE - bp, D_MODEL), F32)
    c_all = jnp.concatenate([c_prompt, pad, c_sample], axis=0)
    rows = _ada_call(c_all, ada_w, ada_b.reshape(DEPTH, 1, -1), tn=N_ADA * D_MODEL // 4)
    kvrows = _ada_call(c_all, kv_ada_w[None], kv_ada_b.reshape(1, 1, -1), tn=2 * D_MODEL)

    def prompt_attn(q, j, lam_init, kb, vb):
        return _attn_prompt_call(q, kb, vb, b_lambda, subln_g, j, lam_init, ATTN_TILE, ATTN_HEADS_PER_STEP)

    y_p, conv_p, h_p, k_p, v_p = _run_trunk(
        x_prompt, _group_rows(rows, C_ROWS_PROMPT, bp), _group_rows(kvrows, C_ROWS_PROMPT, bp),
        jnp.zeros((N_A_LAYERS, bp, SUBLANES, D_RNN), F32), jnp.zeros((N_A_LAYERS, bp, 1, D_RNN), F32),
        0, w, 1, PROMPT_TILE, prompt_attn, BF16)

    n_phys = cache_k.shape[0]
    ck = jnp.transpose(cache_k, (0, 2, 3, 4, 1)).reshape(n_phys, D_MODEL, PAGE_SIZE)
    cv = cache_v.reshape(n_phys, PAGE_SIZE * N_HEADS, V_DIM)
    past_len = page_table.shape[1] * PAGE_SIZE

    def sample_attn(q, j, lam_init, kb, vb):
        padn = ((0, 0), (0, PAGE_SIZE - ss), (0, 0))
        return _attn_decode_call(q, ck, cv, page_table, jnp.pad(kb, padn), jnp.pad(vb, padn),
                                 b_lambda, subln_g, j, lam_init, PAGES_PER_STEP)

    conv0 = jnp.pad(state_conv, ((0, 0), (0, 0), (SUBLANES - (CONV_WIDTH - 1), 0), (0, 0)))
    y_s, conv_s, h_s, k_s, v_s = _run_trunk(
        x_sample, _group_rows(rows, C_ROWS_SAMPLE, bs), _group_rows(kvrows, C_ROWS_SAMPLE, bs),
        conv0, state_h[:, :, None, :], past_len, w, bs, ss, sample_attn, F32)

    return (y_p, y_s, conv_p, h_p, k_p, v_p, conv_s, h_s, k_s, v_s)
```

```python
import functools
import math

import jax
import jax.numpy as jnp
from jax import lax
from jax.experimental import pallas as pl
from jax.experimental.pallas import tpu as pltpu

D_MODEL = 1024
DEPTH = 4
N_A_LAYERS = DEPTH // 2
D_RNN = D_MODEL
N_LRU_BLOCKS = 8
LRU_BLOCK = D_RNN // N_LRU_BLOCKS
CONV_WIDTH = 4
LRU_C = 8.0
N_HEADS = 8
HEAD_DIM = D_MODEL // (2 * N_HEADS)
V_DIM = 2 * HEAD_DIM
D_FF = 2816
ROPE_THETA = 10000.0
NORM_EPS = 1e-6
N_ADA = 9
PAGE_SIZE = 128

SUBLANES = 8
LANES = 128
VMEM_LIMIT_BYTES = 56 * 1024 * 1024

LOG2_E = math.log2(math.e)

F32 = jnp.float32
BF16 = jnp.bfloat16


def _params(*sem):
    return pltpu.CompilerParams(dimension_semantics=sem, vmem_limit_bytes=VMEM_LIMIT_BYTES)


def _const_spec(shape, index_map):
    return pl.BlockSpec(shape, index_map, pipeline_mode=pl.Buffered(1))


def _modulate(x, g, shift, scale):
    ms = jnp.mean(x * x, axis=-1, keepdims=True)
    y = x * lax.rsqrt(ms + NORM_EPS)
    return (y * g) * (1.0 + scale) + shift


def _ada_kernel(c_ref, w_ref, b_ref, o_ref):
    c_act = jax.nn.silu(c_ref[...]).astype(BF16)
    o_ref[...] = jnp.dot(c_act, w_ref[...].astype(BF16), preferred_element_type=F32) + b_ref[...]


def _ada_call(c_all, w, b, tn):
    nl, _, n = w.shape
    r = c_all.shape[0]
    return pl.pallas_call(
        _ada_kernel,
        grid=(nl, n // tn),
        in_specs=[
            pl.BlockSpec((r, D_MODEL), lambda l, j: (0, 0)),
            pl.BlockSpec((None, D_MODEL, tn), lambda l, j: (l, 0, j)),
            pl.BlockSpec((None, 1, tn), lambda l, j: (l, 0, j)),
        ],
        out_specs=pl.BlockSpec((None, r, tn), lambda l, j: (l, 0, j)),
        out_shape=jax.ShapeDtypeStruct((nl, r, n), F32),
        compiler_params=_params("arbitrary", "arbitrary"),
        name="ada_rows",
    )(c_all, w, b)


def _ffn_kernel(*refs, attn_out, final, query):
    it = iter(refs)
    x_ref = next(it)
    if attn_out:
        ao_ref, ogt_ref, wo_ref = next(it), next(it), next(it)
    sh_ref, sc_ref, gt_ref, g_ref, wgu_ref, wd_ref = (next(it) for _ in range(6))
    if final:
        fg_ref = next(it)
    if query:
        qsh_ref, qsc_ref, qg_ref, wq_ref, cos_ref, sin_ref = (next(it) for _ in range(6))
    o_ref = next(it)
    x = x_ref[...]
    nb, t, _ = x.shape
    m = nb * t
    if attn_out:
        ao = jnp.dot(ao_ref[...].reshape(m, D_MODEL).astype(BF16), wo_ref[...], preferred_element_type=F32)
        x = x + ogt_ref[...] * ao.reshape(nb, t, D_MODEL)
    h = _modulate(x, g_ref[...], sh_ref[...], sc_ref[...])
    hb = h.reshape(m, D_MODEL).astype(BF16)
    gu = jnp.dot(hb, wgu_ref[...], preferred_element_type=F32)
    a = (jax.nn.silu(gu[:, :D_FF]) * gu[:, D_FF:]).astype(BF16)
    y = jnp.dot(a, wd_ref[...], preferred_element_type=F32).reshape(nb, t, D_MODEL)
    out = x + (0.5 * gt_ref[...]) * y
    if final:
        ms = jnp.mean(out * out, axis=-1, keepdims=True)
        out = (out * lax.rsqrt(ms + NORM_EPS)) * fg_ref[...]
    o_ref[...] = out
    if query:
        q_ref = next(it)
        hq = _modulate(out, qg_ref[...], qsh_ref[...], qsc_ref[...])
        yq = jnp.dot(hq.reshape(m, D_MODEL).astype(BF16), wq_ref[...], preferred_element_type=F32)
        q = _rope(yq.reshape(nb, t, D_MODEL), cos_ref, sin_ref) * (HEAD_DIM ** -0.5 * LOG2_E)
        q_ref[...] = q.astype(q_ref.dtype)


def _mod_spec(row, nb):
    return pl.BlockSpec((None, nb, 1, D_MODEL), lambda b, t: (row, b, 0, 0))


def _tok_spec(nb, t, width=D_MODEL):
    return pl.BlockSpec((nb, t, width), lambda b, i: (b, i, 0))


def _ffn_call(x, mods, row0, norm_g3, g_row, wgu, wd, l, j, nb, t, final_g=None, attn_out=None, query=None):
    bsz, s, _ = x.shape
    in_specs = [_tok_spec(nb, t)]
    args = [x]
    if attn_out is not None:
        o, gate_row, w_o, oj = attn_out
        in_specs += [_tok_spec(nb, t), _mod_spec(gate_row, nb),
                     _const_spec((None, D_MODEL, D_MODEL), lambda b, i: (oj, 0, 0))]
        args += [o, mods, w_o]
    in_specs += [
        _mod_spec(row0, nb), _mod_spec(row0 + 1, nb), _mod_spec(row0 + 2, nb),
        pl.BlockSpec((None, 1, D_MODEL), lambda b, i: (g_row, 0, 0)),
        _const_spec((None, None, D_MODEL, 2 * D_FF), lambda b, i: (l, j, 0, 0)),
        _const_spec((None, None, D_FF, D_MODEL), lambda b, i: (l, j, 0, 0)),
    ]
    args += [mods, mods, mods, norm_g3, wgu, wd]
    if final_g is not None:
        in_specs.append(pl.BlockSpec((1, D_MODEL), lambda b, i: (0, 0)))
        args.append(final_g)
    out_specs = _tok_spec(nb, t)
    out_shape = jax.ShapeDtypeStruct(x.shape, F32)
    if query is not None:
        qrow0, qg_row, w_q, qj, cos, sin, qdtype = query
        rope_spec = pl.BlockSpec((t, LANES), lambda b, i: (i, 0))
        in_specs += [_mod_spec(qrow0, nb), _mod_spec(qrow0 + 1, nb),
                     pl.BlockSpec((None, 1, D_MODEL), lambda b, i: (qg_row, 0, 0)),
                     _const_spec((None, D_MODEL, D_MODEL), lambda b, i: (qj, 0, 0)),
                     rope_spec, rope_spec]
        args += [mods, mods, norm_g3, w_q, cos, sin]
        out_specs = [out_specs, _tok_spec(nb, t)]
        out_shape = [out_shape, jax.ShapeDtypeStruct(x.shape, qdtype)]
    return pl.pallas_call(
        functools.partial(_ffn_kernel, attn_out=attn_out is not None, final=final_g is not None,
                          query=query is not None),
        grid=(bsz // nb, s // t),
        in_specs=in_specs,
        out_specs=out_specs,
        out_shape=out_shape,
        compiler_params=_params("arbitrary", "arbitrary"),
        name="ffn",
    )(*args)


def _shift_rows(x, tail, j, row8):
    t = x.shape[1]
    r = pltpu.roll(x, j, axis=1)
    head = jnp.where(row8 < j, pltpu.roll(tail, j, axis=1), r[:, :SUBLANES])
    if t == SUBLANES:
        return head
    return jnp.concatenate([head, r[:, SUBLANES:]], axis=1)


def _rglru_kernel(x_ref, sh_ref, sc_ref, gt_ref, g_ref, conv0_ref, h0_ref,
                  win_ref, cw_ref, cb_ref, wg_ref, brg_ref, big_ref, lam_ref, wout_ref,
                  xo_ref, convo_ref, ho_ref,
                  tail_sc, h_sc):
    @pl.when(pl.program_id(1) == 0)
    def _():
        tail_sc[...] = conv0_ref[...]
        h_sc[...] = h0_ref[...]

    x = x_ref[...]
    nb, t, _ = x.shape
    m = nb * t
    h = _modulate(x, g_ref[...], sh_ref[...], sc_ref[...])
    xy = jnp.dot(h.reshape(m, D_MODEL).astype(BF16), win_ref[...], preferred_element_type=F32)
    xb = xy[:, :D_RNN].reshape(nb, t, D_RNN)
    yb = jax.nn.gelu(xy[:, D_RNN:])

    tail = tail_sc[...]
    row8 = lax.broadcasted_iota(jnp.int32, (1, SUBLANES, 1), 1)
    cw = cw_ref[...]
    xc = cb_ref[...] + cw[0:1] * _shift_rows(xb, tail, 3, row8)
    xc = xc + cw[1:2] * _shift_rows(xb, tail, 2, row8)
    xc = xc + cw[2:3] * _shift_rows(xb, tail, 1, row8)
    xc = xc + cw[3:4] * xb
    new_tail = xb[:, t - SUBLANES:, :]
    tail_sc[...] = new_tail
    convo_ref[...] = new_tail

    xc2 = xc.reshape(m, D_RNN)
    xcb = xc2.astype(BF16)
    pre = [jnp.dot(xcb[:, n * LRU_BLOCK:(n + 1) * LRU_BLOCK], wg_ref[n], preferred_element_type=F32)
           for n in range(N_LRU_BLOCKS)]
    rpre = jnp.concatenate([p[:, :LRU_BLOCK] for p in pre], axis=1)
    ipre = jnp.concatenate([p[:, LRU_BLOCK:] for p in pre], axis=1)
    r = jax.nn.sigmoid(rpre + brg_ref[...])
    i = jax.nn.sigmoid(ipre + big_ref[...])
    log_a = (-LRU_C * r) * jax.nn.softplus(-lam_ref[...])
    a3 = jnp.exp(log_a).reshape(nb, t, D_RNN)
    th = jnp.tanh(log_a)
    one_minus_a2 = (-2.0 * th) / (1.0 - th)
    u3 = (jnp.sqrt(one_minus_a2) * (i * xc2)).reshape(nb, t, D_RNN)

    hprev = h_sc[...]
    hs_groups = []
    for gi in range(t // SUBLANES):
        rows = slice(gi * SUBLANES, (gi + 1) * SUBLANES)
        ca = a3[:, rows, :]
        cb = u3[:, rows, :]
        for s in (1, 2, 4):
            keep = row8 >= s
            cb = jnp.where(keep, ca * pltpu.roll(cb, s, axis=1) + cb, cb)
            ca = jnp.where(keep, ca * pltpu.roll(ca, s, axis=1), ca)
        hs = ca * hprev + cb
        hs_groups.append(hs)
        hprev = hs[:, SUBLANES - 1:SUBLANES, :]
    h_sc[...] = hprev
    ho_ref[...] = hprev
    hs_all = hs_groups[0] if len(hs_groups) == 1 else jnp.concatenate(hs_groups, axis=1)

    mixed = (hs_all.reshape(m, D_RNN) * yb).astype(BF16)
    out = jnp.dot(mixed, wout_ref[...], preferred_element_type=F32).reshape(nb, t, D_MODEL)
    xo_ref[...] = x + gt_ref[...] * out


def _rglru_call(x, mods, row0, norm_g3, g_row, conv0, h0, w_in, conv_w, conv_b, w_gate, b_rg, b_ig,
                lam, w_out, l, nb, t):
    bsz, s, _ = x.shape
    vec = lambda: pl.BlockSpec((None, 1, D_RNN), lambda b, i: (l, 0, 0))
    return pl.pallas_call(
        _rglru_kernel,
        grid=(bsz // nb, s // t),
        in_specs=[
            _tok_spec(nb, t),
            _mod_spec(row0, nb), _mod_spec(row0 + 1, nb), _mod_spec(row0 + 2, nb),
            pl.BlockSpec((None, 1, D_MODEL), lambda b, i: (g_row, 0, 0)),
            pl.BlockSpec((None, nb, SUBLANES, D_RNN), lambda b, i: (l, b, 0, 0)),
            pl.BlockSpec((None, nb, 1, D_RNN), lambda b, i: (l, b, 0, 0)),
            _const_spec((None, D_MODEL, 2 * D_RNN), lambda b, i: (l, 0, 0)),
            pl.BlockSpec((None, CONV_WIDTH, D_RNN), lambda b, i: (l, 0, 0)),
            vec(),
            _const_spec((None, N_LRU_BLOCKS, LRU_BLOCK, 2 * LRU_BLOCK), lambda b, i: (l, 0, 0, 0)),
            vec(), vec(), vec(),
            _const_spec((None, D_RNN, D_MODEL), lambda b, i: (l, 0, 0)),
        ],
        out_specs=[
            _tok_spec(nb, t),
            pl.BlockSpec((nb, SUBLANES, D_RNN), lambda b, i: (b, 0, 0)),
            pl.BlockSpec((nb, 1, D_RNN), lambda b, i: (b, 0, 0)),
        ],
        out_shape=[
            jax.ShapeDtypeStruct(x.shape, F32),
            jax.ShapeDtypeStruct((bsz, SUBLANES, D_RNN), F32),
            jax.ShapeDtypeStruct((bsz, 1, D_RNN), F32),
        ],
        scratch_shapes=[
            pltpu.VMEM((nb, SUBLANES, D_RNN), F32),
            pltpu.VMEM((nb, 1, D_RNN), F32),
        ],
        compiler_params=_params("arbitrary", "arbitrary"),
        name="rglru",
    )(x, mods, mods, mods, norm_g3, conv0, h0, w_in, conv_w, conv_b, w_gate, b_rg, b_ig, lam, w_out)


def _rope(x, cos_ref, sin_ref):
    reps = D_MODEL // LANES
    cos = jnp.concatenate([cos_ref[...]] * reps, axis=-1)[None]
    sin = jnp.concatenate([sin_ref[...]] * reps, axis=-1)[None]
    lane = lax.broadcasted_iota(jnp.int32, (1, 1, D_MODEL), 2)
    low_half = (lane % HEAD_DIM) < (HEAD_DIM // 2)
    swapped = jnp.where(low_half,
                        pltpu.roll(x, D_MODEL - HEAD_DIM // 2, axis=2),
                        pltpu.roll(x, HEAD_DIM // 2, axis=2))
    return x * cos + swapped * sin


def _kv_kernel(x_ref, sh_ref, sc_ref, g_ref, w_ref, cos_ref, sin_ref, k_ref, v_ref, kb_ref, vb_ref):
    x = x_ref[...]
    nb, t, _ = x.shape
    h = _modulate(x, g_ref[...], sh_ref[...], sc_ref[...])
    y = jnp.dot(h.reshape(nb * t, D_MODEL).astype(BF16), w_ref[...], preferred_element_type=F32)
    k = _rope(y[:, :D_MODEL].reshape(nb, t, D_MODEL), cos_ref, sin_ref)
    v = y[:, D_MODEL:].reshape(nb, t, D_MODEL)
    k_ref[...] = k
    v_ref[...] = v
    kb_ref[...] = k.astype(BF16)
    vb_ref[...] = v.astype(BF16)


def _kv_call(x, kvmods, kv_norm_g, w_kv, cos, sin, nb, t):
    bsz, s, _ = x.shape
    rope_spec = pl.BlockSpec((t, LANES), lambda b, i: (i, 0))
    shp = lambda dt: jax.ShapeDtypeStruct(x.shape, dt)
    return pl.pallas_call(
        _kv_kernel,
        grid=(bsz // nb, s // t),
        in_specs=[_tok_spec(nb, t), _mod_spec(0, nb), _mod_spec(1, nb),
                  pl.BlockSpec((1, D_MODEL), lambda b, i: (0, 0)),
                  _const_spec((D_MODEL, 2 * D_MODEL), lambda b, i: (0, 0)),
                  rope_spec, rope_spec],
        out_specs=[_tok_spec(nb, t)] * 4,
        out_shape=[shp(F32), shp(F32), shp(BF16), shp(BF16)],
        compiler_params=_params("arbitrary", "arbitrary"),
        name="kv_proj",
    )(x, kvmods, kvmods, kv_norm_g, w_kv, cos, sin)


def _diff_lambda(lamp_ref, lam_init):
    lp = lamp_ref[...]
    e1 = jnp.exp(jnp.sum(lp[0:1] * lp[1:2], axis=-1, keepdims=True))
    e2 = jnp.exp(jnp.sum(lp[2:3] * lp[3:4], axis=-1, keepdims=True))
    return e1 - e2 + lam_init


def _subln(d, sg, lam_init):
    ms = jnp.mean(d * d, axis=-1, keepdims=True)
    return ((d * lax.rsqrt(ms + NORM_EPS)) * sg) * (1.0 - lam_init)


def _attn_prompt_kernel(q_ref, k_ref, v_ref, lamp_ref, sg_ref, o_ref, m_sc, acc_sc, *, lam_init, tq, hg):
    qi = pl.program_id(2)
    lane = lax.broadcasted_iota(jnp.int32, (1, V_DIM), 1)
    qs = []
    for hd in range(hg):
        q = q_ref[:, hd * V_DIM:(hd + 1) * V_DIM]
        zero = jnp.zeros_like(q)
        qs += [jnp.where(lane < HEAD_DIM, q, zero), jnp.where(lane >= HEAD_DIM, q, zero)]
    ones = jnp.ones((tq, V_DIM), BF16)

    nt = (((1,), (1,)), ((), ()))

    def step(off):
        scores = [lax.dot_general(qs[c], k_ref[pl.ds(off, tq), (c // 2) * V_DIM:(c // 2 + 1) * V_DIM], nt,
                                  preferred_element_type=F32) for c in range(2 * hg)]
        for c, s in enumerate(scores):
            rows = slice(c * tq, (c + 1) * tq)
            hd = c // 2
            vb = jnp.concatenate([v_ref[pl.ds(off, tq), hd * V_DIM:(hd + 1) * V_DIM], ones], axis=1)
            m_prev = m_sc[rows]
            m_new = jnp.maximum(m_prev, jnp.max(s, axis=-1, keepdims=True))
            alpha = jnp.exp2(m_prev - m_new)
            p = jnp.exp2(s - jnp.concatenate([m_new] * (tq // LANES), axis=1))
            pv = jnp.dot(p.astype(BF16), vb, preferred_element_type=F32)
            acc_sc[rows] = jnp.concatenate([alpha, alpha], axis=1) * acc_sc[rows] + pv
            m_sc[rows] = m_new

    def diagonal_step(off):
        hq = tq // 2
        pieces = []
        for c in range(2 * hg):
            pieces += [(c, slice(0, hq), hq), (c, slice(hq, tq), tq)]
        scores = [lax.dot_general(qs[c][qr], k_ref[pl.ds(off, nk), (c // 2) * V_DIM:(c // 2 + 1) * V_DIM], nt,
                                  preferred_element_type=F32) for c, qr, nk in pieces]
        for (c, qr, nk), s in zip(pieces, scores):
            hd = c // 2
            r = lax.broadcasted_iota(jnp.int32, (hq, nk), 0) + qr.start
            col = lax.broadcasted_iota(jnp.int32, (hq, nk), 1)
            s = jnp.where(col <= r, s, -jnp.inf)
            m_new = jnp.broadcast_to(jnp.max(s, axis=-1, keepdims=True), (hq, LANES))
            p = jnp.exp2(s - jnp.concatenate([m_new] * (nk // LANES), axis=1))
            vb = jnp.concatenate([v_ref[pl.ds(off, nk), hd * V_DIM:(hd + 1) * V_DIM], ones[:nk]], axis=1)
            rows = slice(c * tq + qr.start, c * tq + qr.stop)
            acc_sc[rows] = jnp.dot(p.astype(BF16), vb, preferred_element_type=F32)
            m_sc[rows] = m_new

    diagonal_step(pl.multiple_of(qi * tq, tq))

    def body(j, carry):
        step(pl.multiple_of(2 * j * tq, tq))
        step(pl.multiple_of((2 * j + 1) * tq, tq))
        return carry

    lax.fori_loop(0, qi // 2, body, 0)

    @pl.when(qi % 2 == 1)
    def _():
        step(pl.multiple_of((qi - 1) * tq, tq))

    lam = _diff_lambda(lamp_ref, lam_init)
    for hd in range(hg):
        acc = acc_sc[2 * hd * tq:(2 * hd + 2) * tq]
        o = acc[:, :V_DIM] / acc[:, V_DIM:]
        d = o[:tq] - lam * o[tq:]
        o_ref[:, hd * V_DIM:(hd + 1) * V_DIM] = _subln(d, sg_ref[...], lam_init).astype(o_ref.dtype)


def _attn_prompt_call(q, k, v, lam_p, subln_g, j, lam_init, tq, hg):
    bsz, s, _ = q.shape
    return pl.pallas_call(
        functools.partial(_attn_prompt_kernel, lam_init=lam_init, tq=tq, hg=hg),
        grid=(bsz, N_HEADS // hg, s // tq),
        in_specs=[
            pl.BlockSpec((None, tq, hg * V_DIM), lambda b, h, i: (b, i, h)),
            pl.BlockSpec((None, s, hg * V_DIM), lambda b, h, i: (b, 0, h)),
            pl.BlockSpec((None, s, hg * V_DIM), lambda b, h, i: (b, 0, h)),
            pl.BlockSpec((None, 4, HEAD_DIM), lambda b, h, i: (j, 0, 0)),
            pl.BlockSpec((None, 1, V_DIM), lambda b, h, i: (j, 0, 0)),
        ],
        out_specs=pl.BlockSpec((None, tq, hg * V_DIM), lambda b, h, i: (b, i, h)),
        out_shape=jax.ShapeDtypeStruct(q.shape, BF16),
        scratch_shapes=[
            pltpu.VMEM((hg * 2 * tq, LANES), F32),
            pltpu.VMEM((hg * 2 * tq, 2 * V_DIM), F32),
        ],
        compiler_params=_params("arbitrary", "arbitrary", "arbitrary"),
        name="attn_prompt",
    )(q, k, v, lam_p, subln_g)


def _attn_decode_kernel(pt_ref, q_ref, *refs, pps, nstep, lam_init):
    k_refs = refs[:pps]
    v_refs = refs[pps:2 * pps]
    kn_ref, vn_ref, lamp_ref, sg_ref, o_ref, m_sc, l_sc, acc_sc, wt_sc = refs[2 * pps:]
    step_id = pl.program_id(1)
    nrow = 2 * N_HEADS * SUBLANES

    @pl.when(step_id == 0)
    def _():
        m_sc[...] = jnp.full(m_sc.shape, -jnp.inf, F32)
        l_sc[...] = jnp.zeros(l_sc.shape, F32)
        acc_sc[...] = jnp.zeros(acc_sc.shape, F32)
        q = q_ref[...]
        row = lax.broadcasted_iota(jnp.int32, (nrow, D_MODEL), 0)
        col = lax.broadcasted_iota(jnp.int32, (nrow, D_MODEL), 1)
        wt_sc[...] = jnp.where((row // SUBLANES) == (col // HEAD_DIM),
                               jnp.concatenate([q] * (nrow // SUBLANES), axis=0), 0.0).astype(BF16)

    hrows = 2 * SUBLANES

    def update(s, v_of):
        m_prev = m_sc[...]
        m_new = jnp.maximum(m_prev, jnp.max(s, axis=-1, keepdims=True))
        alpha = jnp.exp2(m_prev - m_new)
        n_pg = s.shape[1] // PAGE_SIZE
        p = jnp.exp2(s - jnp.concatenate([m_new] * n_pg, axis=1))
        l_sc[...] = alpha * l_sc[...] + jnp.sum(p, axis=-1, keepdims=True)
        p = p.astype(BF16)
        heads = []
        for hd in range(N_HEADS):
            ph = p[hd * hrows:(hd + 1) * hrows]
            pv = jnp.dot(ph[:, :PAGE_SIZE], v_of(0, hd), preferred_element_type=F32)
            for i in range(1, n_pg):
                pv = pv + jnp.dot(ph[:, i * PAGE_SIZE:(i + 1) * PAGE_SIZE], v_of(i, hd),
                                  preferred_element_type=F32)
            heads.append(pv)
        acc_sc[...] = alpha * acc_sc[...] + jnp.concatenate(heads, axis=0)
        m_sc[...] = m_new

    @pl.when(step_id < nstep)
    def _():
        wt = wt_sc[...]
        ss = [jnp.dot(wt, k_refs[i][...].astype(BF16), preferred_element_type=F32) for i in range(pps)]
        vh = [jnp.transpose(v_refs[i][...].astype(BF16).reshape(PAGE_SIZE, N_HEADS, V_DIM), (1, 0, 2))
              for i in range(pps)]
        update(jnp.concatenate(ss, axis=1), lambda i, hd: vh[i][hd])

    @pl.when(step_id == nstep)
    def _():
        s = lax.dot_general(wt_sc[...], kn_ref[...], (((1,), (1,)), ((), ())), preferred_element_type=F32)
        r = lax.broadcasted_iota(jnp.int32, (nrow, PAGE_SIZE), 0)
        c = lax.broadcasted_iota(jnp.int32, (nrow, PAGE_SIZE), 1)
        update(jnp.where(c <= (r % SUBLANES), s, -jnp.inf),
               lambda i, hd: vn_ref[:, hd * V_DIM:(hd + 1) * V_DIM])
        o = acc_sc[...] / l_sc[...]
        lam = _diff_lambda(lamp_ref, lam_init)
        heads = []
        for hd in range(N_HEADS):
            r0 = hd * hrows
            d = o[r0:r0 + SUBLANES] - lam * o[r0 + SUBLANES:r0 + hrows]
            heads.append(_subln(d, sg_ref[...], lam_init))
        o_ref[...] = jnp.concatenate(heads, axis=1)


def _attn_decode_call(q, cache_kt, cache_v, page_table, kn_pad, vn_pad, lam_p, subln_g, j, lam_init, pps):
    bsz = q.shape[0]
    n_pages = page_table.shape[1]
    nstep = n_pages // pps

    def page_spec(i, rows, cols):
        return pl.BlockSpec(
            (None, rows, cols),
            lambda b, s, pt: (pt[b, jnp.minimum(s * pps + i, n_pages - 1)], 0, 0),
        )

    tok = pl.BlockSpec((None, SUBLANES, D_MODEL), lambda b, s, pt: (b, 0, 0))
    new = pl.BlockSpec((None, PAGE_SIZE, D_MODEL), lambda b, s, pt: (b, 0, 0))
    nrow = 2 * N_HEADS * SUBLANES
    grid_spec = pltpu.PrefetchScalarGridSpec(
        num_scalar_prefetch=1,
        grid=(bsz, nstep + 1),
        in_specs=[tok] + [page_spec(i, D_MODEL, PAGE_SIZE) for i in range(pps)]
        + [page_spec(i, PAGE_SIZE * N_HEADS, V_DIM) for i in range(pps)] + [
            new, new,
            pl.BlockSpec((None, 4, HEAD_DIM), lambda b, s, pt: (j, 0, 0)),
            pl.BlockSpec((None, 1, V_DIM), lambda b, s, pt: (j, 0, 0)),
        ],
        out_specs=tok,
        scratch_shapes=[
            pltpu.VMEM((nrow, LANES), F32),
            pltpu.VMEM((nrow, LANES), F32),
            pltpu.VMEM((nrow, V_DIM), F32),
            pltpu.VMEM((nrow, D_MODEL), BF16),
        ],
    )
    return pl.pallas_call(
        functools.partial(_attn_decode_kernel, pps=pps, nstep=nstep, lam_init=lam_init),
        grid_spec=grid_spec,
        out_shape=jax.ShapeDtypeStruct(q.shape, F32),
        compiler_params=_params("arbitrary", "arbitrary"),
        name="attn_decode",
    )(page_table, q, *([cache_kt] * pps), *([cache_v] * pps), kn_pad, vn_pad, lam_p, subln_g)


PROMPT_TILE = 512
ATTN_TILE = 512
ATTN_HEADS_PER_STEP = 4
PAGES_PER_STEP = 16
C_ROWS_PROMPT = 0
C_ROWS_SAMPLE = 8


def _rope_tables(pos):
    half = HEAD_DIM // 2
    inv = 1.0 / (ROPE_THETA ** (jnp.arange(half, dtype=F32) * (2.0 / HEAD_DIM)))
    ang = pos.astype(F32)[:, None] * inv[None, :]
    cos = jnp.tile(jnp.cos(ang), (1, LANES // half))
    sign = jnp.where((jnp.arange(LANES) % HEAD_DIM) < half, -1.0, 1.0).astype(F32)
    sin = jnp.tile(jnp.sin(ang), (1, LANES // half)) * sign[None, :]
    return cos, sin


def _group_rows(rows, lo, n):
    nl, _, kd = rows.shape
    k = kd // D_MODEL
    g = rows[:, lo:lo + n].reshape(nl, n, k, D_MODEL)
    return jnp.transpose(g, (0, 2, 1, 3)).reshape(nl * k, n, 1, D_MODEL)


def _run_trunk(x, mods, kvmods, conv0, h0, past_len, w, nb, t, attn_fn, q_dtype):
    bsz, s, _ = x.shape
    pos = past_len + jnp.arange(s, dtype=jnp.int32)
    cos, sin = _rope_tables(pos)
    convs, hs = [], []
    k_new = v_new = kb = vb = None
    for l in range(DEPTH):
        if l == N_A_LAYERS:
            k_new, v_new, kb, vb = _kv_call(x, kvmods, w["kv_norm_g"], w["w_kv"], cos, sin, nb, t)
        r0 = l * N_ADA
        if l < N_A_LAYERS:
            x = _ffn_call(x, mods, r0, w["norm_g"], l * 3, w["ffn_w_gu"], w["ffn_w_d"], l, 0, nb, t)
            x, cs, hl = _rglru_call(x, mods, r0 + 3, w["norm_g"], l * 3 + 1, conv0, h0,
                                    w["a_w_in"], w["a_conv_w"], w["a_conv_b"], w["a_w_gate"],
                                    w["a_b_rg"], w["a_b_ig"], w["a_lambda"], w["a_w_out"], l, nb, t)
            convs.append(cs[:, SUBLANES - (CONV_WIDTH - 1):])
            hs.append(hl[:, 0])
            attn_out = None
        else:
            j = l - N_A_LAYERS
            lam_init = 0.8 - 0.6 * math.exp(-0.3 * l)
            x, q = _ffn_call(x, mods, r0, w["norm_g"], l * 3, w["ffn_w_gu"], w["ffn_w_d"], l, 0, nb, t,
                             query=(r0 + 3, l * 3 + 1, w["b_w_q"], j, cos, sin, q_dtype))
            attn_out = (attn_fn(q, j, lam_init, kb, vb), r0 + 5, w["b_w_o"], j)
        x = _ffn_call(x, mods, r0 + 6, w["norm_g"], l * 3 + 2, w["ffn_w_gu"], w["ffn_w_d"], l, 1, nb, t,
                      final_g=w["final_g"] if l == DEPTH - 1 else None, attn_out=attn_out)
    k_new = k_new.reshape(bsz, s, N_HEADS, 2, HEAD_DIM)
    v_new = v_new.reshape(bsz, s, N_HEADS, V_DIM)
    return x, jnp.stack(convs, 0), jnp.stack(hs, 0), k_new, v_new


def kernel(x_prompt, x_sample, c_prompt, c_sample, state_conv, state_h, cache_k, cache_v, page_table, ada_w, ada_b, norm_g, ffn_w_gu, ffn_w_d, a_w_in, a_conv_w, a_conv_b, a_w_rg, a_b_rg, a_w_ig, a_b_ig, a_lambda, a_w_out, kv_ada_w, kv_ada_b, kv_norm_g, w_kv, b_w_q, b_lambda, b_subln_g, b_w_o, final_g):
    bp, sp, _ = x_prompt.shape
    bs, ss, _ = x_sample.shape
    assert ss == SUBLANES and sp % PROMPT_TILE == 0 and PROMPT_TILE == ATTN_TILE

    w = {
        "norm_g": norm_g.reshape(DEPTH * 3, 1, D_MODEL),
        "ffn_w_gu": ffn_w_gu.astype(BF16),
        "ffn_w_d": ffn_w_d.astype(BF16),
        "a_w_in": a_w_in.astype(BF16),
        "a_conv_w": a_conv_w,
        "a_conv_b": a_conv_b.reshape(N_A_LAYERS, 1, D_RNN),
        "a_w_gate": jnp.concatenate([a_w_rg, a_w_ig], axis=-1).astype(BF16),
        "a_b_rg": a_b_rg.reshape(N_A_LAYERS, 1, D_RNN),
        "a_b_ig": a_b_ig.reshape(N_A_LAYERS, 1, D_RNN),
        "a_lambda": a_lambda.reshape(N_A_LAYERS, 1, D_RNN),
        "a_w_out": a_w_out.astype(BF16),
        "kv_norm_g": kv_norm_g.reshape(1, D_MODEL),
        "w_kv": w_kv.astype(BF16),
        "b_w_q": b_w_q.astype(BF16),
        "b_w_o": b_w_o.astype(BF16),
        "final_g": final_g.reshape(1, D_MODEL),
    }
    subln_g = b_subln_g.reshape(-1, 1, V_DIM)

    pad = jnp.zeros((C_ROWS_SAMPLE - bp, D_MODEL), F32)
    c_all = jnp.concatenate([c_prompt, pad, c_sample], axis=0)
    rows = _ada_call(c_all, ada_w, ada_b.reshape(DEPTH, 1, -1), tn=N_ADA * D_MODEL // 4)
    kvrows = _ada_call(c_all, kv_ada_w[None], kv_ada_b.reshape(1, 1, -1), tn=2 * D_MODEL)

    def prompt_attn(q, j, lam_init, kb, vb):
        return _attn_prompt_call(q, kb, vb, b_lambda, subln_g, j, lam_init, ATTN_TILE, ATTN_HEADS_PER_STEP)

    y_p, conv_p, h_p, k_p, v_p = _run_trunk(
        x_prompt, _group_rows(rows, C_ROWS_PROMPT, bp), _group_rows(kvrows, C_ROWS_PROMPT, bp),
        jnp.zeros((N_A_LAYERS, bp, SUBLANES, D_RNN), F32), jnp.zeros((N_A_LAYERS, bp, 1, D_RNN), F32),
        0, w, 1, PROMPT_TILE, prompt_attn, BF16)

    n_phys = cache_k.shape[0]
    ck = jnp.transpose(cache_k, (0, 2, 3, 4, 1)).reshape(n_phys, D_MODEL, PAGE_SIZE)
    cv = cache_v.reshape(n_phys, PAGE_SIZE * N_HEADS, V_DIM)
    past_len = page_table.shape[1] * PAGE_SIZE

    def sample_attn(q, j, lam_init, kb, vb):
        padn = ((0, 0), (0, PAGE_SIZE - ss), (0, 0))
        return _attn_decode_call(q, ck, cv, page_table, jnp.pad(kb, padn), jnp.pad(vb, padn),
                                 b_lambda, subln_g, j, lam_init, PAGES_PER_STEP)

    conv0 = jnp.pad(state_conv, ((0, 0), (0, 0), (SUBLANES - (CONV_WIDTH - 1), 0), (0, 0)))
    y_s, conv_s, h_s, k_s, v_s = _run_trunk(
        x_sample, _group_rows(rows, C_ROWS_SAMPLE, bs), _group_rows(kvrows, C_ROWS_SAMPLE, bs),
        conv0, state_h[:, :, None, :], past_len, w, bs, ss, sample_attn, F32)

    return (y_p, y_s, conv_p, h_p, k_p, v_p, conv_s, h_s, k_s, v_s)
```

```python
import functools
import math

import jax
import jax.numpy as jnp
from jax import lax
from jax.experimental import pallas as pl
from jax.experimental.pallas import tpu as pltpu

D_MODEL = 1024
DEPTH = 4
N_A_LAYERS = DEPTH // 2
D_RNN = D_MODEL
N_LRU_BLOCKS = 8
LRU_BLOCK = D_RNN // N_LRU_BLOCKS
CONV_WIDTH = 4
LRU_C = 8.0
N_HEADS = 8
HEAD_DIM = D_MODEL // (2 * N_HEADS)
V_DIM = 2 * HEAD_DIM
D_FF = 2816
ROPE_THETA = 10000.0
NORM_EPS = 1e-6
N_ADA = 9
PAGE_SIZE = 128

SUBLANES = 8
LANES = 128
VMEM_LIMIT_BYTES = 56 * 1024 * 1024

LOG2_E = math.log2(math.e)

F32 = jnp.float32
BF16 = jnp.bfloat16


def _params(*sem):
    return pltpu.CompilerParams(dimension_semantics=sem, vmem_limit_bytes=VMEM_LIMIT_BYTES)


def _const_spec(shape, index_map):
    return pl.BlockSpec(shape, index_map, pipeline_mode=pl.Buffered(1))


def _modulate(x, g, shift, scale):
    ms = jnp.mean(x * x, axis=-1, keepdims=True)
    y = x * lax.rsqrt(ms + NORM_EPS)
    return (y * g) * (1.0 + scale) + shift


def _ada_kernel(c_ref, w_ref, b_ref, o_ref):
    c_act = jax.nn.silu(c_ref[...]).astype(BF16)
    o_ref[...] = jnp.dot(c_act, w_ref[...].astype(BF16), preferred_element_type=F32) + b_ref[...]


def _ada_call(c_all, w, b, tn):
    nl, _, n = w.shape
    r = c_all.shape[0]
    return pl.pallas_call(
        _ada_kernel,
        grid=(nl, n // tn),
        in_specs=[
            pl.BlockSpec((r, D_MODEL), lambda l, j: (0, 0)),
            pl.BlockSpec((None, D_MODEL, tn), lambda l, j: (l, 0, j)),
            pl.BlockSpec((None, 1, tn), lambda l, j: (l, 0, j)),
        ],
        out_specs=pl.BlockSpec((None, r, tn), lambda l, j: (l, 0, j)),
        out_shape=jax.ShapeDtypeStruct((nl, r, n), F32),
        compiler_params=_params("arbitrary", "arbitrary"),
        name="ada_rows",
    )(c_all, w, b)


def _ffn_kernel(*refs, attn_out, final, query):
    it = iter(refs)
    x_ref = next(it)
    if attn_out:
        ao_ref, ogt_ref, wo_ref = next(it), next(it), next(it)
    sh_ref, sc_ref, gt_ref, g_ref, wgu_ref, wd_ref = (next(it) for _ in range(6))
    if final:
        fg_ref = next(it)
    if query:
        qsh_ref, qsc_ref, qg_ref, wq_ref, cos_ref, sin_ref = (next(it) for _ in range(6))
    o_ref = next(it)
    x = x_ref[...]
    nb, t, _ = x.shape
    m = nb * t
    if attn_out:
        ao = jnp.dot(ao_ref[...].reshape(m, D_MODEL).astype(BF16), wo_ref[...], preferred_element_type=F32)
        x = x + ogt_ref[...] * ao.reshape(nb, t, D_MODEL)
    h = _modulate(x, g_ref[...], sh_ref[...], sc_ref[...])
    hb = h.reshape(m, D_MODEL).astype(BF16)
    gu = jnp.dot(hb, wgu_ref[...], preferred_element_type=F32)
    a = (jax.nn.silu(gu[:, :D_FF]) * gu[:, D_FF:]).astype(BF16)
    y = jnp.dot(a, wd_ref[...], preferred_element_type=F32).reshape(nb, t, D_MODEL)
    out = x + (0.5 * gt_ref[...]) * y
    if final:
        ms = jnp.mean(out * out, axis=-1, keepdims=True)
        out = (out * lax.rsqrt(ms + NORM_EPS)) * fg_ref[...]
    o_ref[...] = out
    if query:
        q_ref = next(it)
        hq = _modulate(out, qg_ref[...], qsh_ref[...], qsc_ref[...])
        yq = jnp.dot(hq.reshape(m, D_MODEL).astype(BF16), wq_ref[...], preferred_element_type=F32)
        q = _rope(yq.reshape(nb, t, D_MODEL), cos_ref, sin_ref) * (HEAD_DIM ** -0.5 * LOG2_E)
        q_ref[...] = q.astype(q_ref.dtype)


def _mod_spec(row, nb):
    return pl.BlockSpec((None, nb, 1, D_MODEL), lambda b, t: (row, b, 0, 0))


def _tok_spec(nb, t, width=D_MODEL):
    return pl.BlockSpec((nb, t, width), lambda b, i: (b, i, 0))


def _ffn_call(x, mods, row0, norm_g3, g_row, wgu, wd, l, j, nb, t, final_g=None, attn_out=None, query=None):
    bsz, s, _ = x.shape
    in_specs = [_tok_spec(nb, t)]
    args = [x]
    if attn_out is not None:
        o, gate_row, w_o, oj = attn_out
        in_specs += [_tok_spec(nb, t), _mod_spec(gate_row, nb),
                     _const_spec((None, D_MODEL, D_MODEL), lambda b, i: (oj, 0, 0))]
        args += [o, mods, w_o]
    in_specs += [
        _mod_spec(row0, nb), _mod_spec(row0 + 1, nb), _mod_spec(row0 + 2, nb),
        pl.BlockSpec((None, 1, D_MODEL), lambda b, i: (g_row, 0, 0)),
        _const_spec((None, None, D_MODEL, 2 * D_FF), lambda b, i: (l, j, 0, 0)),
        _const_spec((None, None, D_FF, D_MODEL), lambda b, i: (l, j, 0, 0)),
    ]
    args += [mods, mods, mods, norm_g3, wgu, wd]
    if final_g is not None:
        in_specs.append(pl.BlockSpec((1, D_MODEL), lambda b, i: (0, 0)))
        args.append(final_g)
    out_specs = _tok_spec(nb, t)
    out_shape = jax.ShapeDtypeStruct(x.shape, F32)
    if query is not None:
        qrow0, qg_row, w_q, qj, cos, sin, qdtype = query
        rope_spec = pl.BlockSpec((t, LANES), lambda b, i: (i, 0))
        in_specs += [_mod_spec(qrow0, nb), _mod_spec(qrow0 + 1, nb),
                     pl.BlockSpec((None, 1, D_MODEL), lambda b, i: (qg_row, 0, 0)),
                     _const_spec((None, D_MODEL, D_MODEL), lambda b, i: (qj, 0, 0)),
                     rope_spec, rope_spec]
        args += [mods, mods, norm_g3, w_q, cos, sin]
        out_specs = [out_specs, _tok_spec(nb, t)]
        out_shape = [out_shape, jax.ShapeDtypeStruct(x.shape, qdtype)]
    return pl.pallas_call(
        functools.partial(_ffn_kernel, attn_out=attn_out is not None, final=final_g is not None,
                          query=query is not None),
        grid=(bsz // nb, s // t),
        in_specs=in_specs,
        out_specs=out_specs,
        out_shape=out_shape,
        compiler_params=_params("arbitrary", "arbitrary"),
        name="ffn",
    )(*args)


def _shift_rows(x, tail, j, row8):
    t = x.shape[1]
    r = pltpu.roll(x, j, axis=1)
    head = jnp.where(row8 < j, pltpu.roll(tail, j, axis=1), r[:, :SUBLANES])
    if t == SUBLANES:
        return head
    return jnp.concatenate([head, r[:, SUBLANES:]], axis=1)


def _rglru_kernel(x_ref, sh_ref, sc_ref, gt_ref, g_ref, conv0_ref, h0_ref,
                  win_ref, cw_ref, cb_ref, wg_ref, brg_ref, big_ref, lam_ref, wout_ref,
                  xo_ref, convo_ref, ho_ref,
                  tail_sc, h_sc):
    @pl.when(pl.program_id(1) == 0)
    def _():
        tail_sc[...] = conv0_ref[...]
        h_sc[...] = h0_ref[...]

    x = x_ref[...]
    nb, t, _ = x.shape
    m = nb * t
    h = _modulate(x, g_ref[...], sh_ref[...], sc_ref[...])
    xy = jnp.dot(h.reshape(m, D_MODEL).astype(BF16), win_ref[...], preferred_element_type=F32)
    xb = xy[:, :D_RNN].reshape(nb, t, D_RNN)
    yb = jax.nn.gelu(xy[:, D_RNN:])

    tail = tail_sc[...]
    row8 = lax.broadcasted_iota(jnp.int32, (1, SUBLANES, 1), 1)
    cw = cw_ref[...]
    xc = cb_ref[...] + cw[0:1] * _shift_rows(xb, tail, 3, row8)
    xc = xc + cw[1:2] * _shift_rows(xb, tail, 2, row8)
    xc = xc + cw[2:3] * _shift_rows(xb, tail, 1, row8)
    xc = xc + cw[3:4] * xb
    new_tail = xb[:, t - SUBLANES:, :]
    tail_sc[...] = new_tail
    convo_ref[...] = new_tail

    xc2 = xc.reshape(m, D_RNN)
    xcb = xc2.astype(BF16)
    pre = [jnp.dot(xcb[:, n * LRU_BLOCK:(n + 1) * LRU_BLOCK], wg_ref[n], preferred_element_type=F32)
           for n in range(N_LRU_BLOCKS)]
    rpre = jnp.concatenate([p[:, :LRU_BLOCK] for p in pre], axis=1)
    ipre = jnp.concatenate([p[:, LRU_BLOCK:] for p in pre], axis=1)
    r = jax.nn.sigmoid(rpre + brg_ref[...])
    i = jax.nn.sigmoid(ipre + big_ref[...])
    log_a = (-LRU_C * r) * jax.nn.softplus(-lam_ref[...])
    a3 = jnp.exp(log_a).reshape(nb, t, D_RNN)
    th = jnp.tanh(log_a)
    one_minus_a2 = (-2.0 * th) / (1.0 - th)
    u3 = (jnp.sqrt(one_minus_a2) * (i * xc2)).reshape(nb, t, D_RNN)

    hprev = h_sc[...]
    hs_groups = []
    for gi in range(t // SUBLANES):
        rows = slice(gi * SUBLANES, (gi + 1) * SUBLANES)
        ca = a3[:, rows, :]
        cb = u3[:, rows, :]
        for s in (1, 2, 4):
            keep = row8 >= s
            cb = jnp.where(keep, ca * pltpu.roll(cb, s, axis=1) + cb, cb)
            ca = jnp.where(keep, ca * pltpu.roll(ca, s, axis=1), ca)
        hs = ca * hprev + cb
        hs_groups.append(hs)
        hprev = hs[:, SUBLANES - 1:SUBLANES, :]
    h_sc[...] = hprev
    ho_ref[...] = hprev
    hs_all = hs_groups[0] if len(hs_groups) == 1 else jnp.concatenate(hs_groups, axis=1)

    mixed = (hs_all.reshape(m, D_RNN) * yb).astype(BF16)
    out = jnp.dot(mixed, wout_ref[...], preferred_element_type=F32).reshape(nb, t, D_MODEL)
    xo_ref[...] = x + gt_ref[...] * out


def _rglru_call(x, mods, row0, norm_g3, g_row, conv0, h0, w_in, conv_w, conv_b, w_gate, b_rg, b_ig,
                lam, w_out, l, nb, t):
    bsz, s, _ = x.shape
    vec = lambda: pl.BlockSpec((None, 1, D_RNN), lambda b, i: (l, 0, 0))
    return pl.pallas_call(
        _rglru_kernel,
        grid=(bsz // nb, s // t),
        in_specs=[
            _tok_spec(nb, t),
            _mod_spec(row0, nb), _mod_spec(row0 + 1, nb), _mod_spec(row0 + 2, nb),
            pl.BlockSpec((None, 1, D_MODEL), lambda b, i: (g_row, 0, 0)),
            pl.BlockSpec((None, nb, SUBLANES, D_RNN), lambda b, i: (l, b, 0, 0)),
            pl.BlockSpec((None, nb, 1, D_RNN), lambda b, i: (l, b, 0, 0)),
            _const_spec((None, D_MODEL, 2 * D_RNN), lambda b, i: (l, 0, 0)),
            pl.BlockSpec((None, CONV_WIDTH, D_RNN), lambda b, i: (l, 0, 0)),
            vec(),
            _const_spec((None, N_LRU_BLOCKS, LRU_BLOCK, 2 * LRU_BLOCK), lambda b, i: (l, 0, 0, 0)),
            vec(), vec(), vec(),
            _const_spec((None, D_RNN, D_MODEL), lambda b, i: (l, 0, 0)),
        ],
        out_specs=[
            _tok_spec(nb, t),
            pl.BlockSpec((nb, SUBLANES, D_RNN), lambda b, i: (b, 0, 0)),
            pl.BlockSpec((nb, 1, D_RNN), lambda b, i: (b, 0, 0)),
        ],
        out_shape=[
            jax.ShapeDtypeStruct(x.shape, F32),
            jax.ShapeDtypeStruct((bsz, SUBLANES, D_RNN), F32),
            jax.ShapeDtypeStruct((bsz, 1, D_RNN), F32),
        ],
        scratch_shapes=[
            pltpu.VMEM((nb, SUBLANES, D_RNN), F32),
            pltpu.VMEM((nb, 1, D_RNN), F32),
        ],
        compiler_params=_params("arbitrary", "arbitrary"),
        name="rglru",
    )(x, mods, mods, mods, norm_g3, conv0, h0, w_in, conv_w, conv_b, w_gate, b_rg, b_ig, lam, w_out)


def _rope(x, cos_ref, sin_ref):
    reps = D_MODEL // LANES
    cos = jnp.concatenate([cos_ref[...]] * reps, axis=-1)[None]
    sin = jnp.concatenate([sin_ref[...]] * reps, axis=-1)[None]
    lane = lax.broadcasted_iota(jnp.int32, (1, 1, D_MODEL), 2)
    low_half = (lane % HEAD_DIM) < (HEAD_DIM // 2)
    swapped = jnp.where(low_half,
                        pltpu.roll(x, D_MODEL - HEAD_DIM // 2, axis=2),
                        pltpu.roll(x, HEAD_DIM // 2, axis=2))
    return x * cos + swapped * sin


def _kv_kernel(x_ref, sh_ref, sc_ref, g_ref, w_ref, cos_ref, sin_ref, k_ref, v_ref, kb_ref, vb_ref):
    x = x_ref[...]
    nb, t, _ = x.shape
    h = _modulate(x, g_ref[...], sh_ref[...], sc_ref[...])
    y = jnp.dot(h.reshape(nb * t, D_MODEL).astype(BF16), w_ref[...], preferred_element_type=F32)
    k = _rope(y[:, :D_MODEL].reshape(nb, t, D_MODEL), cos_ref, sin_ref)
    v = y[:, D_MODEL:].reshape(nb, t, D_MODEL)
    k_ref[...] = k
    v_ref[...] = v
    kb_ref[...] = k.astype(BF16)
    vb_ref[...] = v.astype(BF16)


def _kv_call(x, kvmods, kv_norm_g, w_kv, cos, sin, nb, t):
    bsz, s, _ = x.shape
    rope_spec = pl.BlockSpec((t, LANES), lambda b, i: (i, 0))
    shp = lambda dt: jax.ShapeDtypeStruct(x.shape, dt)
    return pl.pallas_call(
        _kv_kernel,
        grid=(bsz // nb, s // t),
        in_specs=[_tok_spec(nb, t), _mod_spec(0, nb), _mod_spec(1, nb),
                  pl.BlockSpec((1, D_MODEL), lambda b, i: (0, 0)),
                  _const_spec((D_MODEL, 2 * D_MODEL), lambda b, i: (0, 0)),
                  rope_spec, rope_spec],
        out_specs=[_tok_spec(nb, t)] * 4,
        out_shape=[shp(F32), shp(F32), shp(BF16), shp(BF16)],
        compiler_params=_params("arbitrary", "arbitrary"),
        name="kv_proj",
    )(x, kvmods, kvmods, kv_norm_g, w_kv, cos, sin)


def _diff_lambda(lamp_ref, lam_init):
    lp = lamp_ref[...]
    e1 = jnp.exp(jnp.sum(lp[0:1] * lp[1:2], axis=-1, keepdims=True))
    e2 = jnp.exp(jnp.sum(lp[2:3] * lp[3:4], axis=-1, keepdims=True))
    return e1 - e2 + lam_init


def _subln(d, sg, lam_init):
    ms = jnp.mean(d * d, axis=-1, keepdims=True)
    return ((d * lax.rsqrt(ms + NORM_EPS)) * sg) * (1.0 - lam_init)


def _attn_prompt_kernel(q_ref, k_ref, v_ref, lamp_ref, sg_ref, o_ref, m_sc, acc_sc, *, lam_init, tq, hg):
    qi = pl.program_id(2)
    lane = lax.broadcasted_iota(jnp.int32, (1, V_DIM), 1)
    qs = []
    for hd in range(hg):
        q = q_ref[:, hd * V_DIM:(hd + 1) * V_DIM]
        zero = jnp.zeros_like(q)
        qs += [jnp.where(lane < HEAD_DIM, q, zero), jnp.where(lane >= HEAD_DIM, q, zero)]
    ones = jnp.ones((tq, V_DIM), BF16)

    nt = (((1,), (1,)), ((), ()))

    def step(off):
        scores = [lax.dot_general(qs[c], k_ref[pl.ds(off, tq), (c // 2) * V_DIM:(c // 2 + 1) * V_DIM], nt,
                                  preferred_element_type=F32) for c in range(2 * hg)]
        for c, s in enumerate(scores):
            rows = slice(c * tq, (c + 1) * tq)
            hd = c // 2
            vb = jnp.concatenate([v_ref[pl.ds(off, tq), hd * V_DIM:(hd + 1) * V_DIM], ones], axis=1)
            m_prev = m_sc[rows]
            m_new = jnp.maximum(m_prev, jnp.max(s, axis=-1, keepdims=True))
            alpha = jnp.exp2(m_prev - m_new)
            p = jnp.exp2(s - jnp.concatenate([m_new] * (tq // LANES), axis=1))
            pv = jnp.dot(p.astype(BF16), vb, preferred_element_type=F32)
            acc_sc[rows] = jnp.concatenate([alpha, alpha], axis=1) * acc_sc[rows] + pv
            m_sc[rows] = m_new

    def diagonal_step(off):
        hq = tq // 2
        pieces = []
        for c in range(2 * hg):
            pieces += [(c, slice(0, hq), hq), (c, slice(hq, tq), tq)]
        scores = [lax.dot_general(qs[c][qr], k_ref[pl.ds(off, nk), (c // 2) * V_DIM:(c // 2 + 1) * V_DIM], nt,
                                  preferred_element_type=F32) for c, qr, nk in pieces]
        for (c, qr, nk), s in zip(pieces, scores):
            hd = c // 2
            r = lax.broadcasted_iota(jnp.int32, (hq, nk), 0) + qr.start
            col = lax.broadcasted_iota(jnp.int32, (hq, nk), 1)
            s = jnp.where(col <= r, s, -jnp.inf)
            m_new = jnp.broadcast_to(jnp.max(s, axis=-1, keepdims=True), (hq, LANES))
            p = jnp.exp2(s - jnp.concatenate([m_new] * (nk // LANES), axis=1))
            vb = jnp.concatenate([v_ref[pl.ds(off, nk), hd * V_DIM:(hd + 1) * V_DIM], ones[:nk]], axis=1)
            rows = slice(c * tq + qr.start, c * tq + qr.stop)
            acc_sc[rows] = jnp.dot(p.astype(BF16), vb, preferred_element_type=F32)
            m_sc[rows] = m_new

    diagonal_step(pl.multiple_of(qi * tq, tq))

    def body(j, carry):
        step(pl.multiple_of(2 * j * tq, tq))
        step(pl.multiple_of((2 * j + 1) * tq, tq))
        return carry

    lax.fori_loop(0, qi // 2, body, 0)

    @pl.when(qi % 2 == 1)
    def _():
        step(pl.multiple_of((qi - 1) * tq, tq))

    lam = _diff_lambda(lamp_ref, lam_init)
    for hd in range(hg):
        acc = acc_sc[2 * hd * tq:(2 * hd + 2) * tq]
        o = acc[:, :V_DIM] / acc[:, V_DIM:]
        d = o[:tq] - lam * o[tq:]
        o_ref[:, hd * V_DIM:(hd + 1) * V_DIM] = _subln(d, sg_ref[...], lam_init).astype(o_ref.dtype)


def _attn_prompt_call(q, k, v, lam_p, subln_g, j, lam_init, tq, hg):
    bsz, s, _ = q.shape
    return pl.pallas_call(
        functools.partial(_attn_prompt_kernel, lam_init=lam_init, tq=tq, hg=hg),
        grid=(bsz, N_HEADS // hg, s // tq),
        in_specs=[
            pl.BlockSpec((None, tq, hg * V_DIM), lambda b, h, i: (b, i, h)),
            pl.BlockSpec((None, s, hg * V_DIM), lambda b, h, i: (b, 0, h)),
            pl.BlockSpec((None, s, hg * V_DIM), lambda b, h, i: (b, 0, h)),
            pl.BlockSpec((None, 4, HEAD_DIM), lambda b, h, i: (j, 0, 0)),
            pl.BlockSpec((None, 1, V_DIM), lambda b, h, i: (j, 0, 0)),
        ],
        out_specs=pl.BlockSpec((None, tq, hg * V_DIM), lambda b, h, i: (b, i, h)),
        out_shape=jax.ShapeDtypeStruct(q.shape, BF16),
        scratch_shapes=[
            pltpu.VMEM((hg * 2 * tq, LANES), F32),
            pltpu.VMEM((hg * 2 * tq, 2 * V_DIM), F32),
        ],
        compiler_params=_params("arbitrary", "arbitrary", "arbitrary"),
        name="attn_prompt",
    )(q, k, v, lam_p, subln_g)


def _attn_decode_kernel(pt_ref, q_ref, *refs, pps, nstep, lam_init):
    k_refs = refs[:pps]
    v_refs = refs[pps:2 * pps]
    kn_ref, vn_ref, lamp_ref, sg_ref, o_ref, m_sc, l_sc, acc_sc, wt_sc = refs[2 * pps:]
    step_id = pl.program_id(1)
    nrow = 2 * N_HEADS * SUBLANES

    @pl.when(step_id == 0)
    def _():
        m_sc[...] = jnp.full(m_sc.shape, -jnp.inf, F32)
        l_sc[...] = jnp.zeros(l_sc.shape, F32)
        acc_sc[...] = jnp.zeros(acc_sc.shape, F32)
        q = q_ref[...]
        row = lax.broadcasted_iota(jnp.int32, (nrow, D_MODEL), 0)
        col = lax.broadcasted_iota(jnp.int32, (nrow, D_MODEL), 1)
        wt_sc[...] = jnp.where((row // SUBLANES) == (col // HEAD_DIM),
                               jnp.concatenate([q] * (nrow // SUBLANES), axis=0), 0.0).astype(BF16)

    hrows = 2 * SUBLANES

    def update(s, v_of):
        m_prev = m_sc[...]
        m_new = jnp.maximum(m_prev, jnp.max(s, axis=-1, keepdims=True))
        alpha = jnp.exp2(m_prev - m_new)
        n_pg = s.shape[1] // PAGE_SIZE
        p = jnp.exp2(s - jnp.concatenate([m_new] * n_pg, axis=1))
        l_sc[...] = alpha * l_sc[...] + jnp.sum(p, axis=-1, keepdims=True)
        p = p.astype(BF16)
        heads = []
        for hd in range(N_HEADS):
            ph = p[hd * hrows:(hd + 1) * hrows]
            pv = jnp.dot(ph[:, :PAGE_SIZE], v_of(0, hd), preferred_element_type=F32)
            for i in range(1, n_pg):
                pv = pv + jnp.dot(ph[:, i * PAGE_SIZE:(i + 1) * PAGE_SIZE], v_of(i, hd),
                                  preferred_element_type=F32)
            heads.append(pv)
        acc_sc[...] = alpha * acc_sc[...] + jnp.concatenate(heads, axis=0)
        m_sc[...] = m_new

    @pl.when(step_id < nstep)
    def _():
        wt = wt_sc[...]
        ss = [jnp.dot(wt, k_refs[i][...].astype(BF16), preferred_element_type=F32) for i in range(pps)]
        vh = [jnp.transpose(v_refs[i][...].astype(BF16).reshape(PAGE_SIZE, N_HEADS, V_DIM), (1, 0, 2))
              for i in range(pps)]
        update(jnp.concatenate(ss, axis=1), lambda i, hd: vh[i][hd])

    @pl.when(step_id == nstep)
    def _():
        s = lax.dot_general(wt_sc[...], kn_ref[...], (((1,), (1,)), ((), ())), preferred_element_type=F32)
        r = lax.broadcasted_iota(jnp.int32, (nrow, PAGE_SIZE), 0)
        c = lax.broadcasted_iota(jnp.int32, (nrow, PAGE_SIZE), 1)
        update(jnp.where(c <= (r % SUBLANES), s, -jnp.inf),
               lambda i, hd: vn_ref[:, hd * V_DIM:(hd + 1) * V_DIM])
        o = acc_sc[...] / l_sc[...]
        lam = _diff_lambda(lamp_ref, lam_init)
        heads = []
        for hd in range(N_HEADS):
            r0 = hd * hrows
            d = o[r0:r0 + SUBLANES] - lam * o[r0 + SUBLANES:r0 + hrows]
            heads.append(_subln(d, sg_ref[...], lam_init))
        o_ref[...] = jnp.concatenate(heads, axis=1)


def _attn_decode_call(q, cache_kt, cache_v, page_table, kn_pad, vn_pad, lam_p, subln_g, j, lam_init, pps):
    bsz = q.shape[0]
    n_pages = page_table.shape[1]
    nstep = n_pages // pps

    def page_spec(i, rows, cols):
        return pl.BlockSpec(
            (None, rows, cols),
            lambda b, s, pt: (pt[b, jnp.minimum(s * pps + i, n_pages - 1)], 0, 0),
        )

    tok = pl.BlockSpec((None, SUBLANES, D_MODEL), lambda b, s, pt: (b, 0, 0))
    new = pl.BlockSpec((None, PAGE_SIZE, D_MODEL), lambda b, s, pt: (b, 0, 0))
    nrow = 2 * N_HEADS * SUBLANES
    grid_spec = pltpu.PrefetchScalarGridSpec(
        num_scalar_prefetch=1,
        grid=(bsz, nstep + 1),
        in_specs=[tok] + [page_spec(i, D_MODEL, PAGE_SIZE) for i in range(pps)]
        + [page_spec(i, PAGE_SIZE * N_HEADS, V_DIM) for i in range(pps)] + [
            new, new,
            pl.BlockSpec((None, 4, HEAD_DIM), lambda b, s, pt: (j, 0, 0)),
            pl.BlockSpec((None, 1, V_DIM), lambda b, s, pt: (j, 0, 0)),
        ],
        out_specs=tok,
        scratch_shapes=[
            pltpu.VMEM((nrow, LANES), F32),
            pltpu.VMEM((nrow, LANES), F32),
            pltpu.VMEM((nrow, V_DIM), F32),
            pltpu.VMEM((nrow, D_MODEL), BF16),
        ],
    )
    return pl.pallas_call(
        functools.partial(_attn_decode_kernel, pps=pps, nstep=nstep, lam_init=lam_init),
        grid_spec=grid_spec,
        out_shape=jax.ShapeDtypeStruct(q.shape, F32),
        compiler_params=_params("arbitrary", "arbitrary"),
        name="attn_decode",
    )(page_table, q, *([cache_kt] * pps), *([cache_v] * pps), kn_pad, vn_pad, lam_p, subln_g)


PROMPT_TILE = 512
ATTN_TILE = 512
ATTN_HEADS_PER_STEP = 4
PAGES_PER_STEP = 8
C_ROWS_PROMPT = 0
C_ROWS_SAMPLE = 8


def _rope_tables(pos):
    half = HEAD_DIM // 2
    inv = 1.0 / (ROPE_THETA ** (jnp.arange(half, dtype=F32) * (2.0 / HEAD_DIM)))
    ang = pos.astype(F32)[:, None] * inv[None, :]
    cos = jnp.tile(jnp.cos(ang), (1, LANES // half))
    sign = jnp.where((jnp.arange(LANES) % HEAD_DIM) < half, -1.0, 1.0).astype(F32)
    sin = jnp.tile(jnp.sin(ang), (1, LANES // half)) * sign[None, :]
    return cos, sin


def _group_rows(rows, lo, n):
    nl, _, kd = rows.shape
    k = kd // D_MODEL
    g = rows[:, lo:lo + n].reshape(nl, n, k, D_MODEL)
    return jnp.transpose(g, (0, 2, 1, 3)).reshape(nl * k, n, 1, D_MODEL)


def _run_trunk(x, mods, kvmods, conv0, h0, past_len, w, nb, t, attn_fn, q_dtype):
    bsz, s, _ = x.shape
    pos = past_len + jnp.arange(s, dtype=jnp.int32)
    cos, sin = _rope_tables(pos)
    convs, hs = [], []
    k_new = v_new = kb = vb = None
    for l in range(DEPTH):
        if l == N_A_LAYERS:
            k_new, v_new, kb, vb = _kv_call(x, kvmods, w["kv_norm_g"], w["w_kv"], cos, sin, nb, t)
        r0 = l * N_ADA
        if l < N_A_LAYERS:
            x = _ffn_call(x, mods, r0, w["norm_g"], l * 3, w["ffn_w_gu"], w["ffn_w_d"], l, 0, nb, t)
            x, cs, hl = _rglru_call(x, mods, r0 + 3, w["norm_g"], l * 3 + 1, conv0, h0,
                                    w["a_w_in"], w["a_conv_w"], w["a_conv_b"], w["a_w_gate"],
                                    w["a_b_rg"], w["a_b_ig"], w["a_lambda"], w["a_w_out"], l, nb, t)
            convs.append(cs[:, SUBLANES - (CONV_WIDTH - 1):])
            hs.append(hl[:, 0])
            attn_out = None
        else:
            j = l - N_A_LAYERS
            lam_init = 0.8 - 0.6 * math.exp(-0.3 * l)
            x, q = _ffn_call(x, mods, r0, w["norm_g"], l * 3, w["ffn_w_gu"], w["ffn_w_d"], l, 0, nb, t,
                             query=(r0 + 3, l * 3 + 1, w["b_w_q"], j, cos, sin, q_dtype))
            attn_out = (attn_fn(q, j, lam_init, kb, vb), r0 + 5, w["b_w_o"], j)
        x = _ffn_call(x, mods, r0 + 6, w["norm_g"], l * 3 + 2, w["ffn_w_gu"], w["ffn_w_d"], l, 1, nb, t,
                      final_g=w["final_g"] if l == DEPTH - 1 else None, attn_out=attn_out)
    k_new = k_new.reshape(bsz, s, N_HEADS, 2, HEAD_DIM)
    v_new = v_new.reshape(bsz, s, N_HEADS, V_DIM)
    return x, jnp.stack(convs, 0), jnp.stack(hs, 0), k_new, v_new


def kernel(x_prompt, x_sample, c_prompt, c_sample, state_conv, state_h, cache_k, cache_v, page_table, ada_w, ada_b, norm_g, ffn_w_gu, ffn_w_d, a_w_in, a_conv_w, a_conv_b, a_w_rg, a_b_rg, a_w_ig, a_b_ig, a_lambda, a_w_out, kv_ada_w, kv_ada_b, kv_norm_g, w_kv, b_w_q, b_lambda, b_subln_g, b_w_o, final_g):
    bp, sp, _ = x_prompt.shape
    bs, ss, _ = x_sample.shape
    assert ss == SUBLANES and sp % PROMPT_TILE == 0 and PROMPT_TILE == ATTN_TILE

    w = {
        "norm_g": norm_g.reshape(DEPTH * 3, 1, D_MODEL),
        "ffn_w_gu": ffn_w_gu.astype(BF16),
        "ffn_w_d": ffn_w_d.astype(BF16),
        "a_w_in": a_w_in.astype(BF16),
        "a_conv_w": a_conv_w,
        "a_conv_b": a_conv_b.reshape(N_A_LAYERS, 1, D_RNN),
        "a_w_gate": jnp.concatenate([a_w_rg, a_w_ig], axis=-1).astype(BF16),
        "a_b_rg": a_b_rg.reshape(N_A_LAYERS, 1, D_RNN),
        "a_b_ig": a_b_ig.reshape(N_A_LAYERS, 1, D_RNN),
        "a_lambda": a_lambda.reshape(N_A_LAYERS, 1, D_RNN),
        "a_w_out": a_w_out.astype(BF16),
        "kv_norm_g": kv_norm_g.reshape(1, D_MODEL),
        "w_kv": w_kv.astype(BF16),
        "b_w_q": b_w_q.astype(BF16),
        "b_w_o": b_w_o.astype(BF16),
        "final_g": final_g.reshape(1, D_MODEL),
    }
    subln_g = b_subln_g.reshape(-1, 1, V_DIM)

    pad = jnp.zeros((C_ROWS_SAMPLE - bp, D_MODEL), F32)
    c_all = jnp.concatenate([c_prompt, pad, c_sample], axis=0)
    rows = _ada_call(c_all, ada_w, ada_b.reshape(DEPTH, 1, -1), tn=N_ADA * D_MODEL // 4)
    kvrows = _ada_call(c_all, kv_ada_w[None], kv_ada_b.reshape(1, 1, -1), tn=2 * D_MODEL)

    def prompt_attn(q, j, lam_init, kb, vb):
        return _attn_prompt_call(q, kb, vb, b_lambda, subln_g, j, lam_init, ATTN_TILE, ATTN_HEADS_PER_STEP)

    y_p, conv_p, h_p, k_p, v_p = _run_trunk(
        x_prompt, _group_rows(rows, C_ROWS_PROMPT, bp), _group_rows(kvrows, C_ROWS_PROMPT, bp),
        jnp.zeros((N_A_LAYERS, bp, SUBLANES, D_RNN), F32), jnp.zeros((N_A_LAYERS, bp, 1, D_RNN), F32),
        0, w, 1, PROMPT_TILE, prompt_attn, BF16)

    n_phys = cache_k.shape[0]
    ck = jnp.transpose(cache_k, (0, 2, 3, 4, 1)).reshape(n_phys, D_MODEL, PAGE_SIZE)
    cv = cache_v.reshape(n_phys, PAGE_SIZE * N_HEADS, V_DIM)
    past_len = page_table.shape[1] * PAGE_SIZE

    def sample_attn(q, j, lam_init, kb, vb):
        padn = ((0, 0), (0, PAGE_SIZE - ss), (0, 0))
        return _attn_decode_call(q, ck, cv, page_table, jnp.pad(kb, padn), jnp.pad(vb, padn),
                                 b_lambda, subln_g, j, lam_init, PAGES_PER_STEP)

    conv0 = jnp.pad(state_conv, ((0, 0), (0, 0), (SUBLANES - (CONV_WIDTH - 1), 0), (0, 0)))
    y_s, conv_s, h_s, k_s, v_s = _run_trunk(
        x_sample, _group_rows(rows, C_ROWS_SAMPLE, bs), _group_rows(kvrows, C_ROWS_SAMPLE, bs),
        conv0, state_h[:, :, None, :], past_len, w, bs, ss, sample_attn, F32)

    return (y_p, y_s, conv_p, h_p, k_p, v_p, conv_s, h_s, k_s, v_s)
```

```python
import functools
import math

import jax
import jax.numpy as jnp
from jax import lax
from jax.experimental import pallas as pl
from jax.experimental.pallas import tpu as pltpu

D_MODEL = 1024
DEPTH = 4
N_A_LAYERS = DEPTH // 2
D_RNN = D_MODEL
N_LRU_BLOCKS = 8
LRU_BLOCK = D_RNN // N_LRU_BLOCKS
CONV_WIDTH = 4
LRU_C = 8.0
N_HEADS = 8
HEAD_DIM = D_MODEL // (2 * N_HEADS)
V_DIM = 2 * HEAD_DIM
D_FF = 2816
ROPE_THETA = 10000.0
NORM_EPS = 1e-6
N_ADA = 9
PAGE_SIZE = 128

SUBLANES = 8
LANES = 128
VMEM_LIMIT_BYTES = 56 * 1024 * 1024

LOG2_E = math.log2(math.e)

F32 = jnp.float32
BF16 = jnp.bfloat16


def _params(*sem):
    return pltpu.CompilerParams(dimension_semantics=sem, vmem_limit_bytes=VMEM_LIMIT_BYTES)


def _const_spec(shape, index_map):
    return pl.BlockSpec(shape, index_map, pipeline_mode=pl.Buffered(1))


def _modulate(x, g, shift, scale):
    ms = jnp.mean(x * x, axis=-1, keepdims=True)
    y = x * lax.rsqrt(ms + NORM_EPS)
    return (y * g) * (1.0 + scale) + shift


def _ada_kernel(c_ref, w_ref, b_ref, o_ref):
    c_act = jax.nn.silu(c_ref[...]).astype(BF16)
    o_ref[...] = jnp.dot(c_act, w_ref[...].astype(BF16), preferred_element_type=F32) + b_ref[...]


def _ada_call(c_all, w, b, tn):
    nl, _, n = w.shape
    r = c_all.shape[0]
    return pl.pallas_call(
        _ada_kernel,
        grid=(nl, n // tn),
        in_specs=[
            pl.BlockSpec((r, D_MODEL), lambda l, j: (0, 0)),
            pl.BlockSpec((None, D_MODEL, tn), lambda l, j: (l, 0, j)),
            pl.BlockSpec((None, 1, tn), lambda l, j: (l, 0, j)),
        ],
        out_specs=pl.BlockSpec((None, r, tn), lambda l, j: (l, 0, j)),
        out_shape=jax.ShapeDtypeStruct((nl, r, n), F32),
        compiler_params=_params("arbitrary", "arbitrary"),
        name="ada_rows",
    )(c_all, w, b)


def _ffn_kernel(*refs, attn_out, final, query):
    it = iter(refs)
    x_ref = next(it)
    if attn_out:
        ao_ref, ogt_ref, wo_ref = next(it), next(it), next(it)
    sh_ref, sc_ref, gt_ref, g_ref, wgu_ref, wd_ref = (next(it) for _ in range(6))
    if final:
        fg_ref = next(it)
    if query:
        qsh_ref, qsc_ref, qg_ref, wq_ref, cos_ref, sin_ref = (next(it) for _ in range(6))
    o_ref = next(it)
    x = x_ref[...]
    nb, t, _ = x.shape
    m = nb * t
    if attn_out:
        ao = jnp.dot(ao_ref[...].reshape(m, D_MODEL).astype(BF16), wo_ref[...], preferred_element_type=F32)
        x = x + ogt_ref[...] * ao.reshape(nb, t, D_MODEL)
    h = _modulate(x, g_ref[...], sh_ref[...], sc_ref[...])
    hb = h.reshape(m, D_MODEL).astype(BF16)
    gu = jnp.dot(hb, wgu_ref[...], preferred_element_type=F32)
    a = (jax.nn.silu(gu[:, :D_FF]) * gu[:, D_FF:]).astype(BF16)
    y = jnp.dot(a, wd_ref[...], preferred_element_type=F32).reshape(nb, t, D_MODEL)
    out = x + (0.5 * gt_ref[...]) * y
    if final:
        ms = jnp.mean(out * out, axis=-1, keepdims=True)
        out = (out * lax.rsqrt(ms + NORM_EPS)) * fg_ref[...]
    o_ref[...] = out
    if query:
        q_ref = next(it)
        hq = _modulate(out, qg_ref[...], qsh_ref[...], qsc_ref[...])
        yq = jnp.dot(hq.reshape(m, D_MODEL).astype(BF16), wq_ref[...], preferred_element_type=F32)
        q = _rope(yq.reshape(nb, t, D_MODEL), cos_ref, sin_ref) * (HEAD_DIM ** -0.5 * LOG2_E)
        q_ref[...] = q.astype(q_ref.dtype)


def _mod_spec(row, nb):
    return pl.BlockSpec((None, nb, 1, D_MODEL), lambda b, t: (row, b, 0, 0))


def _tok_spec(nb, t, width=D_MODEL):
    return pl.BlockSpec((nb, t, width), lambda b, i: (b, i, 0))


def _ffn_call(x, mods, row0, norm_g3, g_row, wgu, wd, l, j, nb, t, final_g=None, attn_out=None, query=None):
    bsz, s, _ = x.shape
    in_specs = [_tok_spec(nb, t)]
    args = [x]
    if attn_out is not None:
        o, gate_row, w_o, oj = attn_out
        in_specs += [_tok_spec(nb, t), _mod_spec(gate_row, nb),
                     _const_spec((None, D_MODEL, D_MODEL), lambda b, i: (oj, 0, 0))]
        args += [o, mods, w_o]
    in_specs += [
        _mod_spec(row0, nb), _mod_spec(row0 + 1, nb), _mod_spec(row0 + 2, nb),
        pl.BlockSpec((None, 1, D_MODEL), lambda b, i: (g_row, 0, 0)),
        _const_spec((None, None, D_MODEL, 2 * D_FF), lambda b, i: (l, j, 0, 0)),
        _const_spec((None, None, D_FF, D_MODEL), lambda b, i: (l, j, 0, 0)),
    ]
    args += [mods, mods, mods, norm_g3, wgu, wd]
    if final_g is not None:
        in_specs.append(pl.BlockSpec((1, D_MODEL), lambda b, i: (0, 0)))
        args.append(final_g)
    out_specs = _tok_spec(nb, t)
    out_shape = jax.ShapeDtypeStruct(x.shape, F32)
    if query is not None:
        qrow0, qg_row, w_q, qj, cos, sin, qdtype = query
        rope_spec = pl.BlockSpec((t, LANES), lambda b, i: (i, 0))
        in_specs += [_mod_spec(qrow0, nb), _mod_spec(qrow0 + 1, nb),
                     pl.BlockSpec((None, 1, D_MODEL), lambda b, i: (qg_row, 0, 0)),
                     _const_spec((None, D_MODEL, D_MODEL), lambda b, i: (qj, 0, 0)),
                     rope_spec, rope_spec]
        args += [mods, mods, norm_g3, w_q, cos, sin]
        out_specs = [out_specs, _tok_spec(nb, t)]
        out_shape = [out_shape, jax.ShapeDtypeStruct(x.shape, qdtype)]
    return pl.pallas_call(
        functools.partial(_ffn_kernel, attn_out=attn_out is not None, final=final_g is not None,
                          query=query is not None),
        grid=(bsz // nb, s // t),
        in_specs=in_specs,
        out_specs=out_specs,
        out_shape=out_shape,
        compiler_params=_params("arbitrary", "arbitrary"),
        name="ffn",
    )(*args)


def _shift_rows(x, tail, j, row8):
    t = x.shape[1]
    r = pltpu.roll(x, j, axis=1)
    head = jnp.where(row8 < j, pltpu.roll(tail, j, axis=1), r[:, :SUBLANES])
    if t == SUBLANES:
        return head
    return jnp.concatenate([head, r[:, SUBLANES:]], axis=1)


def _rglru_kernel(x_ref, sh_ref, sc_ref, gt_ref, g_ref, conv0_ref, h0_ref,
                  win_ref, cw_ref, cb_ref, wg_ref, brg_ref, big_ref, lam_ref, wout_ref,
                  xo_ref, convo_ref, ho_ref,
                  tail_sc, h_sc):
    @pl.when(pl.program_id(1) == 0)
    def _():
        tail_sc[...] = conv0_ref[...]
        h_sc[...] = h0_ref[...]

    x = x_ref[...]
    nb, t, _ = x.shape
    m = nb * t
    h = _modulate(x, g_ref[...], sh_ref[...], sc_ref[...])
    xy = jnp.dot(h.reshape(m, D_MODEL).astype(BF16), win_ref[...], preferred_element_type=F32)
    xb = xy[:, :D_RNN].reshape(nb, t, D_RNN)
    yb = jax.nn.gelu(xy[:, D_RNN:])

    tail = tail_sc[...]
    row8 = lax.broadcasted_iota(jnp.int32, (1, SUBLANES, 1), 1)
    cw = cw_ref[...]
    xc = cb_ref[...] + cw[0:1] * _shift_rows(xb, tail, 3, row8)
    xc = xc + cw[1:2] * _shift_rows(xb, tail, 2, row8)
    xc = xc + cw[2:3] * _shift_rows(xb, tail, 1, row8)
    xc = xc + cw[3:4] * xb
    new_tail = xb[:, t - SUBLANES:, :]
    tail_sc[...] = new_tail
    convo_ref[...] = new_tail

    xc2 = xc.reshape(m, D_RNN)
    xcb = xc2.astype(BF16)
    pre = [jnp.dot(xcb[:, n * LRU_BLOCK:(n + 1) * LRU_BLOCK], wg_ref[n], preferred_element_type=F32)
           for n in range(N_LRU_BLOCKS)]
    rpre = jnp.concatenate([p[:, :LRU_BLOCK] for p in pre], axis=1)
    ipre = jnp.concatenate([p[:, LRU_BLOCK:] for p in pre], axis=1)
    r = jax.nn.sigmoid(rpre + brg_ref[...])
    i = jax.nn.sigmoid(ipre + big_ref[...])
    log_a = (-LRU_C * r) * jax.nn.softplus(-lam_ref[...])
    a3 = jnp.exp(log_a).reshape(nb, t, D_RNN)
    th = jnp.tanh(log_a)
    one_minus_a2 = (-2.0 * th) / (1.0 - th)
    u3 = (jnp.sqrt(one_minus_a2) * (i * xc2)).reshape(nb, t, D_RNN)

    hprev = h_sc[...]
    hs_groups = []
    for gi in range(t // SUBLANES):
        rows = slice(gi * SUBLANES, (gi + 1) * SUBLANES)
        ca = a3[:, rows, :]
        cb = u3[:, rows, :]
        for s in (1, 2, 4):
            keep = row8 >= s
            cb = jnp.where(keep, ca * pltpu.roll(cb, s, axis=1) + cb, cb)
            ca = jnp.where(keep, ca * pltpu.roll(ca, s, axis=1), ca)
        hs = ca * hprev + cb
        hs_groups.append(hs)
        hprev = hs[:, SUBLANES - 1:SUBLANES, :]
    h_sc[...] = hprev
    ho_ref[...] = hprev
    hs_all = hs_groups[0] if len(hs_groups) == 1 else jnp.concatenate(hs_groups, axis=1)

    mixed = (hs_all.reshape(m, D_RNN) * yb).astype(BF16)
    out = jnp.dot(mixed, wout_ref[...], preferred_element_type=F32).reshape(nb, t, D_MODEL)
    xo_ref[...] = x + gt_ref[...] * out


def _rglru_call(x, mods, row0, norm_g3, g_row, conv0, h0, w_in, conv_w, conv_b, w_gate, b_rg, b_ig,
                lam, w_out, l, nb, t):
    bsz, s, _ = x.shape
    vec = lambda: pl.BlockSpec((None, 1, D_RNN), lambda b, i: (l, 0, 0))
    return pl.pallas_call(
        _rglru_kernel,
        grid=(bsz // nb, s // t),
        in_specs=[
            _tok_spec(nb, t),
            _mod_spec(row0, nb), _mod_spec(row0 + 1, nb), _mod_spec(row0 + 2, nb),
            pl.BlockSpec((None, 1, D_MODEL), lambda b, i: (g_row, 0, 0)),
            pl.BlockSpec((None, nb, SUBLANES, D_RNN), lambda b, i: (l, b, 0, 0)),
            pl.BlockSpec((None, nb, 1, D_RNN), lambda b, i: (l, b, 0, 0)),
            _const_spec((None, D_MODEL, 2 * D_RNN), lambda b, i: (l, 0, 0)),
            pl.BlockSpec((None, CONV_WIDTH, D_RNN), lambda b, i: (l, 0, 0)),
            vec(),
            _const_spec((None, N_LRU_BLOCKS, LRU_BLOCK, 2 * LRU_BLOCK), lambda b, i: (l, 0, 0, 0)),
            vec(), vec(), vec(),
            _const_spec((None, D_RNN, D_MODEL), lambda b, i: (l, 0, 0)),
        ],
        out_specs=[
            _tok_spec(nb, t),
            pl.BlockSpec((nb, SUBLANES, D_RNN), lambda b, i: (b, 0, 0)),
            pl.BlockSpec((nb, 1, D_RNN), lambda b, i: (b, 0, 0)),
        ],
        out_shape=[
            jax.ShapeDtypeStruct(x.shape, F32),
            jax.ShapeDtypeStruct((bsz, SUBLANES, D_RNN), F32),
            jax.ShapeDtypeStruct((bsz, 1, D_RNN), F32),
        ],
        scratch_shapes=[
            pltpu.VMEM((nb, SUBLANES, D_RNN), F32),
            pltpu.VMEM((nb, 1, D_RNN), F32),
        ],
        compiler_params=_params("arbitrary", "arbitrary"),
        name="rglru",
    )(x, mods, mods, mods, norm_g3, conv0, h0, w_in, conv_w, conv_b, w_gate, b_rg, b_ig, lam, w_out)


def _rope(x, cos_ref, sin_ref):
    reps = D_MODEL // LANES
    cos = jnp.concatenate([cos_ref[...]] * reps, axis=-1)[None]
    sin = jnp.concatenate([sin_ref[...]] * reps, axis=-1)[None]
    lane = lax.broadcasted_iota(jnp.int32, (1, 1, D_MODEL), 2)
    low_half = (lane % HEAD_DIM) < (HEAD_DIM // 2)
    swapped = jnp.where(low_half,
                        pltpu.roll(x, D_MODEL - HEAD_DIM // 2, axis=2),
                        pltpu.roll(x, HEAD_DIM // 2, axis=2))
    return x * cos + swapped * sin


def _kv_kernel(x_ref, sh_ref, sc_ref, g_ref, w_ref, cos_ref, sin_ref, k_ref, v_ref, kb_ref, vb_ref):
    x = x_ref[...]
    nb, t, _ = x.shape
    h = _modulate(x, g_ref[...], sh_ref[...], sc_ref[...])
    y = jnp.dot(h.reshape(nb * t, D_MODEL).astype(BF16), w_ref[...], preferred_element_type=F32)
    k = _rope(y[:, :D_MODEL].reshape(nb, t, D_MODEL), cos_ref, sin_ref)
    v = y[:, D_MODEL:].reshape(nb, t, D_MODEL)
    k_ref[...] = k
    v_ref[...] = v
    kb_ref[...] = k.astype(BF16)
    vb_ref[...] = v.astype(BF16)


def _kv_call(x, kvmods, kv_norm_g, w_kv, cos, sin, nb, t):
    bsz, s, _ = x.shape
    rope_spec = pl.BlockSpec((t, LANES), lambda b, i: (i, 0))
    shp = lambda dt: jax.ShapeDtypeStruct(x.shape, dt)
    return pl.pallas_call(
        _kv_kernel,
        grid=(bsz // nb, s // t),
        in_specs=[_tok_spec(nb, t), _mod_spec(0, nb), _mod_spec(1, nb),
                  pl.BlockSpec((1, D_MODEL), lambda b, i: (0, 0)),
                  _const_spec((D_MODEL, 2 * D_MODEL), lambda b, i: (0, 0)),
                  rope_spec, rope_spec],
        out_specs=[_tok_spec(nb, t)] * 4,
        out_shape=[shp(F32), shp(F32), shp(BF16), shp(BF16)],
        compiler_params=_params("arbitrary", "arbitrary"),
        name="kv_proj",
    )(x, kvmods, kvmods, kv_norm_g, w_kv, cos, sin)


def _diff_lambda(lamp_ref, lam_init):
    lp = lamp_ref[...]
    e1 = jnp.exp(jnp.sum(lp[0:1] * lp[1:2], axis=-1, keepdims=True))
    e2 = jnp.exp(jnp.sum(lp[2:3] * lp[3:4], axis=-1, keepdims=True))
    return e1 - e2 + lam_init


def _subln(d, sg, lam_init):
    ms = jnp.mean(d * d, axis=-1, keepdims=True)
    return ((d * lax.rsqrt(ms + NORM_EPS)) * sg) * (1.0 - lam_init)


def _attn_prompt_kernel(q_ref, k_ref, v_ref, lamp_ref, sg_ref, o_ref, m_sc, acc_sc, *, lam_init, tq, hg):
    qi = pl.program_id(2)
    lane = lax.broadcasted_iota(jnp.int32, (1, V_DIM), 1)
    qs = []
    for hd in range(hg):
        q = q_ref[:, hd * V_DIM:(hd + 1) * V_DIM]
        zero = jnp.zeros_like(q)
        qs += [jnp.where(lane < HEAD_DIM, q, zero), jnp.where(lane >= HEAD_DIM, q, zero)]
    ones = jnp.ones((tq, V_DIM), BF16)

    nt = (((1,), (1,)), ((), ()))

    def step(off):
        scores = [lax.dot_general(qs[c], k_ref[pl.ds(off, tq), (c // 2) * V_DIM:(c // 2 + 1) * V_DIM], nt,
                                  preferred_element_type=F32) for c in range(2 * hg)]
        for c, s in enumerate(scores):
            rows = slice(c * tq, (c + 1) * tq)
            hd = c // 2
            vb = jnp.concatenate([v_ref[pl.ds(off, tq), hd * V_DIM:(hd + 1) * V_DIM], ones], axis=1)
            m_prev = m_sc[rows]
            m_new = jnp.maximum(m_prev, jnp.max(s, axis=-1, keepdims=True))
            alpha = jnp.exp2(m_prev - m_new)
            p = jnp.exp2(s - jnp.concatenate([m_new] * (tq // LANES), axis=1))
            pv = jnp.dot(p.astype(BF16), vb, preferred_element_type=F32)
            acc_sc[rows] = jnp.concatenate([alpha, alpha], axis=1) * acc_sc[rows] + pv
            m_sc[rows] = m_new

    def diagonal_step(off):
        hq = tq // 2
        pieces = []
        for c in range(2 * hg):
            pieces += [(c, slice(0, hq), hq), (c, slice(hq, tq), tq)]
        scores = [lax.dot_general(qs[c][qr], k_ref[pl.ds(off, nk), (c // 2) * V_DIM:(c // 2 + 1) * V_DIM], nt,
                                  preferred_element_type=F32) for c, qr, nk in pieces]
        for (c, qr, nk), s in zip(pieces, scores):
            hd = c // 2
            r = lax.broadcasted_iota(jnp.int32, (hq, nk), 0) + qr.start
            col = lax.broadcasted_iota(jnp.int32, (hq, nk), 1)
            s = jnp.where(col <= r, s, -jnp.inf)
            m_new = jnp.broadcast_to(jnp.max(s, axis=-1, keepdims=True), (hq, LANES))
            p = jnp.exp2(s - jnp.concatenate([m_new] * (nk // LANES), axis=1))
            vb = jnp.concatenate([v_ref[pl.ds(off, nk), hd * V_DIM:(hd + 1) * V_DIM], ones[:nk]], axis=1)
            rows = slice(c * tq + qr.start, c * tq + qr.stop)
            acc_sc[rows] = jnp.dot(p.astype(BF16), vb, preferred_element_type=F32)
            m_sc[rows] = m_new

    diagonal_step(pl.multiple_of(qi * tq, tq))

    def body(j, carry):
        step(pl.multiple_of(2 * j * tq, tq))
        step(pl.multiple_of((2 * j + 1) * tq, tq))
        return carry

    lax.fori_loop(0, qi // 2, body, 0)

    @pl.when(qi % 2 == 1)
    def _():
        step(pl.multiple_of((qi - 1) * tq, tq))

    lam = _diff_lambda(lamp_ref, lam_init)
    for hd in range(hg):
        acc = acc_sc[2 * hd * tq:(2 * hd + 2) * tq]
        o = acc[:, :V_DIM] / acc[:, V_DIM:]
        d = o[:tq] - lam * o[tq:]
        o_ref[:, hd * V_DIM:(hd + 1) * V_DIM] = _subln(d, sg_ref[...], lam_init).astype(o_ref.dtype)


def _attn_prompt_call(q, k, v, lam_p, subln_g, j, lam_init, tq, hg):
    bsz, s, _ = q.shape
    return pl.pallas_call(
        functools.partial(_attn_prompt_kernel, lam_init=lam_init, tq=tq, hg=hg),
        grid=(bsz, N_HEADS // hg, s // tq),
        in_specs=[
            pl.BlockSpec((None, tq, hg * V_DIM), lambda b, h, i: (b, i, h)),
            pl.BlockSpec((None, s, hg * V_DIM), lambda b, h, i: (b, 0, h)),
            pl.BlockSpec((None, s, hg * V_DIM), lambda b, h, i: (b, 0, h)),
            pl.BlockSpec((None, 4, HEAD_DIM), lambda b, h, i: (j, 0, 0)),
            pl.BlockSpec((None, 1, V_DIM), lambda b, h, i: (j, 0, 0)),
        ],
        out_specs=pl.BlockSpec((None, tq, hg * V_DIM), lambda b, h, i: (b, i, h)),
        out_shape=jax.ShapeDtypeStruct(q.shape, BF16),
        scratch_shapes=[
            pltpu.VMEM((hg * 2 * tq, LANES), F32),
            pltpu.VMEM((hg * 2 * tq, 2 * V_DIM), F32),
        ],
        compiler_params=_params("arbitrary", "arbitrary", "arbitrary"),
        name="attn_prompt",
    )(q, k, v, lam_p, subln_g)


def _attn_decode_kernel(pt_ref, q_ref, *refs, pps, nstep, lam_init):
    k_refs = refs[:pps]
    v_refs = refs[pps:2 * pps]
    kn_ref, vn_ref, lamp_ref, sg_ref, o_ref, m_sc, l_sc, acc_sc, wt_sc = refs[2 * pps:]
    step_id = pl.program_id(1)
    nrow = 2 * N_HEADS * SUBLANES

    @pl.when(step_id == 0)
    def _():
        m_sc[...] = jnp.full(m_sc.shape, -jnp.inf, F32)
        l_sc[...] = jnp.zeros(l_sc.shape, F32)
        acc_sc[...] = jnp.zeros(acc_sc.shape, F32)
        q = q_ref[...]
        row = lax.broadcasted_iota(jnp.int32, (nrow, D_MODEL), 0)
        col = lax.broadcasted_iota(jnp.int32, (nrow, D_MODEL), 1)
        wt_sc[...] = jnp.where((row // SUBLANES) == (col // HEAD_DIM),
                               jnp.concatenate([q] * (nrow // SUBLANES), axis=0), 0.0).astype(BF16)

    hrows = 2 * SUBLANES

    def update(s, v_of):
        m_prev = m_sc[...]
        m_new = jnp.maximum(m_prev, jnp.max(s, axis=-1, keepdims=True))
        alpha = jnp.exp2(m_prev - m_new)
        n_pg = s.shape[1] // PAGE_SIZE
        p = jnp.exp2(s - jnp.concatenate([m_new] * n_pg, axis=1))
        l_sc[...] = alpha * l_sc[...] + jnp.sum(p, axis=-1, keepdims=True)
        p = p.astype(BF16)
        heads = []
        for hd in range(N_HEADS):
            ph = p[hd * hrows:(hd + 1) * hrows]
            pv = jnp.dot(ph[:, :PAGE_SIZE], v_of(0, hd), preferred_element_type=F32)
            for i in range(1, n_pg):
                pv = pv + jnp.dot(ph[:, i * PAGE_SIZE:(i + 1) * PAGE_SIZE], v_of(i, hd),
                                  preferred_element_type=F32)
            heads.append(pv)
        acc_sc[...] = alpha * acc_sc[...] + jnp.concatenate(heads, axis=0)
        m_sc[...] = m_new

    wt = wt_sc[...]
    ss = [jnp.dot(wt, k_refs[i][...].astype(BF16), preferred_element_type=F32) for i in range(pps)]
    vh = [jnp.transpose(v_refs[i][...].astype(BF16).reshape(PAGE_SIZE, N_HEADS, V_DIM), (1, 0, 2))
          for i in range(pps)]
    update(jnp.concatenate(ss, axis=1), lambda i, hd: vh[i][hd])

    @pl.when(step_id == nstep - 1)
    def _():
        s = lax.dot_general(wt_sc[...], kn_ref[...], (((1,), (1,)), ((), ())), preferred_element_type=F32)
        r = lax.broadcasted_iota(jnp.int32, (nrow, PAGE_SIZE), 0)
        c = lax.broadcasted_iota(jnp.int32, (nrow, PAGE_SIZE), 1)
        update(jnp.where(c <= (r % SUBLANES), s, -jnp.inf),
               lambda i, hd: vn_ref[:, hd * V_DIM:(hd + 1) * V_DIM])
        o = acc_sc[...] / l_sc[...]
        lam = _diff_lambda(lamp_ref, lam_init)
        heads = []
        for hd in range(N_HEADS):
            r0 = hd * hrows
            d = o[r0:r0 + SUBLANES] - lam * o[r0 + SUBLANES:r0 + hrows]
            heads.append(_subln(d, sg_ref[...], lam_init))
        o_ref[...] = jnp.concatenate(heads, axis=1)


def _attn_decode_call(q, cache_kt, cache_v, page_table, kn_pad, vn_pad, lam_p, subln_g, j, lam_init, pps):
    bsz = q.shape[0]
    n_pages = page_table.shape[1]
    nstep = n_pages // pps

    def page_spec(i, rows, cols):
        return pl.BlockSpec(
            (None, rows, cols),
            lambda b, s, pt: (pt[b, s * pps + i], 0, 0),
        )

    tok = pl.BlockSpec((None, SUBLANES, D_MODEL), lambda b, s, pt: (b, 0, 0))
    new = pl.BlockSpec((None, PAGE_SIZE, D_MODEL), lambda b, s, pt: (b, 0, 0))
    nrow = 2 * N_HEADS * SUBLANES
    grid_spec = pltpu.PrefetchScalarGridSpec(
        num_scalar_prefetch=1,
        grid=(bsz, nstep),
        in_specs=[tok] + [page_spec(i, D_MODEL, PAGE_SIZE) for i in range(pps)]
        + [page_spec(i, PAGE_SIZE * N_HEADS, V_DIM) for i in range(pps)] + [
            new, new,
            pl.BlockSpec((None, 4, HEAD_DIM), lambda b, s, pt: (j, 0, 0)),
            pl.BlockSpec((None, 1, V_DIM), lambda b, s, pt: (j, 0, 0)),
        ],
        out_specs=tok,
        scratch_shapes=[
            pltpu.VMEM((nrow, LANES), F32),
            pltpu.VMEM((nrow, LANES), F32),
            pltpu.VMEM((nrow, V_DIM), F32),
            pltpu.VMEM((nrow, D_MODEL), BF16),
        ],
    )
    return pl.pallas_call(
        functools.partial(_attn_decode_kernel, pps=pps, nstep=nstep, lam_init=lam_init),
        grid_spec=grid_spec,
        out_shape=jax.ShapeDtypeStruct(q.shape, F32),
        compiler_params=_params("arbitrary", "arbitrary"),
        name="attn_decode",
    )(page_table, q, *([cache_kt] * pps), *([cache_v] * pps), kn_pad, vn_pad, lam_p, subln_g)


PROMPT_TILE = 512
ATTN_TILE = 512
ATTN_HEADS_PER_STEP = 4
PAGES_PER_STEP = 8
C_ROWS_PROMPT = 0
C_ROWS_SAMPLE = 8


def _rope_tables(pos):
    half = HEAD_DIM // 2
    inv = 1.0 / (ROPE_THETA ** (jnp.arange(half, dtype=F32) * (2.0 / HEAD_DIM)))
    ang = pos.astype(F32)[:, None] * inv[None, :]
    cos = jnp.tile(jnp.cos(ang), (1, LANES // half))
    sign = jnp.where((jnp.arange(LANES) % HEAD_DIM) < half, -1.0, 1.0).astype(F32)
    sin = jnp.tile(jnp.sin(ang), (1, LANES // half)) * sign[None, :]
    return cos, sin


def _group_rows(rows, lo, n):
    nl, _, kd = rows.shape
    k = kd // D_MODEL
    g = rows[:, lo:lo + n].reshape(nl, n, k, D_MODEL)
    return jnp.transpose(g, (0, 2, 1, 3)).reshape(nl * k, n, 1, D_MODEL)


def _run_trunk(x, mods, kvmods, conv0, h0, past_len, w, nb, t, attn_fn, q_dtype):
    bsz, s, _ = x.shape
    pos = past_len + jnp.arange(s, dtype=jnp.int32)
    cos, sin = _rope_tables(pos)
    convs, hs = [], []
    k_new = v_new = kb = vb = None
    for l in range(DEPTH):
        if l == N_A_LAYERS:
            k_new, v_new, kb, vb = _kv_call(x, kvmods, w["kv_norm_g"], w["w_kv"], cos, sin, nb, t)
        r0 = l * N_ADA
        if l < N_A_LAYERS:
            x = _ffn_call(x, mods, r0, w["norm_g"], l * 3, w["ffn_w_gu"], w["ffn_w_d"], l, 0, nb, t)
            x, cs, hl = _rglru_call(x, mods, r0 + 3, w["norm_g"], l * 3 + 1, conv0, h0,
                                    w["a_w_in"], w["a_conv_w"], w["a_conv_b"], w["a_w_gate"],
                                    w["a_b_rg"], w["a_b_ig"], w["a_lambda"], w["a_w_out"], l, nb, t)
            convs.append(cs[:, SUBLANES - (CONV_WIDTH - 1):])
            hs.append(hl[:, 0])
            attn_out = None
        else:
            j = l - N_A_LAYERS
            lam_init = 0.8 - 0.6 * math.exp(-0.3 * l)
            x, q = _ffn_call(x, mods, r0, w["norm_g"], l * 3, w["ffn_w_gu"], w["ffn_w_d"], l, 0, nb, t,
                             query=(r0 + 3, l * 3 + 1, w["b_w_q"], j, cos, sin, q_dtype))
            attn_out = (attn_fn(q, j, lam_init, kb, vb), r0 + 5, w["b_w_o"], j)
        x = _ffn_call(x, mods, r0 + 6, w["norm_g"], l * 3 + 2, w["ffn_w_gu"], w["ffn_w_d"], l, 1, nb, t,
                      final_g=w["final_g"] if l == DEPTH - 1 else None, attn_out=attn_out)
    k_new = k_new.reshape(bsz, s, N_HEADS, 2, HEAD_DIM)
    v_new = v_new.reshape(bsz, s, N_HEADS, V_DIM)
    return x, jnp.stack(convs, 0), jnp.stack(hs, 0), k_new, v_new


def kernel(x_prompt, x_sample, c_prompt, c_sample, state_conv, state_h, cache_k, cache_v, page_table, ada_w, ada_b, norm_g, ffn_w_gu, ffn_w_d, a_w_in, a_conv_w, a_conv_b, a_w_rg, a_b_rg, a_w_ig, a_b_ig, a_lambda, a_w_out, kv_ada_w, kv_ada_b, kv_norm_g, w_kv, b_w_q, b_lambda, b_subln_g, b_w_o, final_g):
    bp, sp, _ = x_prompt.shape
    bs, ss, _ = x_sample.shape
    assert ss == SUBLANES and sp % PROMPT_TILE == 0 and PROMPT_TILE == ATTN_TILE

    w = {
        "norm_g": norm_g.reshape(DEPTH * 3, 1, D_MODEL),
        "ffn_w_gu": ffn_w_gu.astype(BF16),
        "ffn_w_d": ffn_w_d.astype(BF16),
        "a_w_in": a_w_in.astype(BF16),
        "a_conv_w": a_conv_w,
        "a_conv_b": a_conv_b.reshape(N_A_LAYERS, 1, D_RNN),
        "a_w_gate": jnp.concatenate([a_w_rg, a_w_ig], axis=-1).astype(BF16),
        "a_b_rg": a_b_rg.reshape(N_A_LAYERS, 1, D_RNN),
        "a_b_ig": a_b_ig.reshape(N_A_LAYERS, 1, D_RNN),
        "a_lambda": a_lambda.reshape(N_A_LAYERS, 1, D_RNN),
        "a_w_out": a_w_out.astype(BF16),
        "kv_norm_g": kv_norm_g.reshape(1, D_MODEL),
        "w_kv": w_kv.astype(BF16),
        "b_w_q": b_w_q.astype(BF16),
        "b_w_o": b_w_o.astype(BF16),
        "final_g": final_g.reshape(1, D_MODEL),
    }
    subln_g = b_subln_g.reshape(-1, 1, V_DIM)

    pad = jnp.zeros((C_ROWS_SAMPLE - bp, D_MODEL), F32)
    c_all = jnp.concatenate([c_prompt, pad, c_sample], axis=0)
    rows = _ada_call(c_all, ada_w, ada_b.reshape(DEPTH, 1, -1), tn=N_ADA * D_MODEL // 4)
    kvrows = _ada_call(c_all, kv_ada_w[None], kv_ada_b.reshape(1, 1, -1), tn=2 * D_MODEL)

    def prompt_attn(q, j, lam_init, kb, vb):
        return _attn_prompt_call(q, kb, vb, b_lambda, subln_g, j, lam_init, ATTN_TILE, ATTN_HEADS_PER_STEP)

    y_p, conv_p, h_p, k_p, v_p = _run_trunk(
        x_prompt, _group_rows(rows, C_ROWS_PROMPT, bp), _group_rows(kvrows, C_ROWS_PROMPT, bp),
        jnp.zeros((N_A_LAYERS, bp, SUBLANES, D_RNN), F32), jnp.zeros((N_A_LAYERS, bp, 1, D_RNN), F32),
        0, w, 1, PROMPT_TILE, prompt_attn, BF16)

    n_phys = cache_k.shape[0]
    ck = jnp.transpose(cache_k, (0, 2, 3, 4, 1)).reshape(n_phys, D_MODEL, PAGE_SIZE)
    cv = cache_v.reshape(n_phys, PAGE_SIZE * N_HEADS, V_DIM)
    past_len = page_table.shape[1] * PAGE_SIZE

    def sample_attn(q, j, lam_init, kb, vb):
        padn = ((0, 0), (0, PAGE_SIZE - ss), (0, 0))
        return _attn_decode_call(q, ck, cv, page_table, jnp.pad(kb, padn), jnp.pad(vb, padn),
                                 b_lambda, subln_g, j, lam_init, PAGES_PER_STEP)

    conv0 = jnp.pad(state_conv, ((0, 0), (0, 0), (SUBLANES - (CONV_WIDTH - 1), 0), (0, 0)))
    y_s, conv_s, h_s, k_s, v_s = _run_trunk(
        x_sample, _group_rows(rows, C_ROWS_SAMPLE, bs), _group_rows(kvrows, C_ROWS_SAMPLE, bs),
        conv0, state_h[:, :, None, :], past_len, w, bs, ss, sample_attn, F32)

    return (y_p, y_s, conv_p, h_p, k_p, v_p, conv_s, h_s, k_s, v_s)
```
